```python
import math
import jax, jax.numpy as jnp
from jax import lax
import numpy as np

D_MODEL = 2048
BATCH = 2
SEQ = 4096
DEPTH = 4
DEC_BATCH = 32
DEC_SEQ = 1
PAST_LEN = 16384
PAGE_SIZE = 128

N_MIXERS = 2
N_A_LAYERS = (DEPTH + N_MIXERS - 1) // N_MIXERS
N_B_LAYERS = DEPTH // N_MIXERS
A_HEADS = 16
A_KV_HEADS = 8
A_GROUP = A_HEADS // A_KV_HEADS
A_HD = D_MODEL // A_HEADS // 2
A_VD = 2 * A_HD
A_Q_COLS = A_HEADS * 2 * A_HD
A_K_COLS = A_KV_HEADS * 2 * A_HD
A_V_COLS = A_KV_HEADS * A_VD
A_QKV_COLS = A_Q_COLS + A_K_COLS + A_V_COLS
A_OUT = A_HEADS * A_VD
B_HEADS = 32
B_KV_HEADS = 4
B_GROUP = B_HEADS // B_KV_HEADS
B_HD = 64
WINDOW = 128
B_Q_COLS = B_HEADS * B_HD
B_K_COLS = B_KV_HEADS * B_HD
B_QKV_COLS = B_Q_COLS + 2 * B_K_COLS
B_OUT = B_HEADS * B_HD
D_FF = -(-8 * D_MODEL // (3 * 256)) * 256
NUM_BUCKETS = 32
MAX_DISTANCE = 128
BIAS_HEADS = 32
Q_BLOCK = 128
RMS_EPS = 1e-6
NEG = -1e30

kernel_name = 'diff_swa_hybrid_step'


def _rms(x, g):
    xf = x.astype(jnp.float32)
    y = xf * lax.rsqrt(jnp.mean(xf * xf, axis=-1, keepdims=True) + RMS_EPS)
    return (y * g.astype(jnp.float32)).astype(x.dtype)


def _t5_bias(dist, table):
    n = jnp.maximum(dist, 0)
    max_exact = NUM_BUCKETS // 2
    nf = jnp.maximum(n, max_exact).astype(jnp.float32)
    large = max_exact + (jnp.log(nf / max_exact) / math.log(MAX_DISTANCE / max_exact)
                         * (NUM_BUCKETS - max_exact)).astype(jnp.int32)
    bucket = jnp.where(n < max_exact, n, jnp.minimum(large, NUM_BUCKETS - 1))
    return table.astype(jnp.float32)[bucket]


def _ada(c, w, b):
    return jnp.split(jax.nn.silu(c) @ w + b, 6, axis=-1)


def _modulate(x, g, shift, scale):
    return _rms(x, g) * (1 + scale[:, None]) + shift[:, None]


def _swiglu(h, w_gu, w_down):
    gate, up = jnp.split(h @ w_gu, 2, axis=-1)
    return (jax.nn.silu(gate) * up) @ w_down


def _diff_qkv(h, w_qkv, qk_g):
    B, T, _ = h.shape
    q, k, v = jnp.split(h @ w_qkv, [A_Q_COLS, A_Q_COLS + A_K_COLS], axis=-1)
    q = _rms(q.reshape(B, T, A_KV_HEADS, A_GROUP, 2, A_HD), qk_g[0])
    k = _rms(k.reshape(B, T, A_KV_HEADS, 2, A_HD), qk_g[1])
    v = v.reshape(B, T, A_KV_HEADS, A_VD)
    return q, k, v


def _diff_lambda(lam_p, lam_init):
    lp = lam_p.astype(jnp.float32)
    return jnp.exp(jnp.sum(lp[0] * lp[1])) - jnp.exp(jnp.sum(lp[2] * lp[3])) + lam_init


def _diff_logits(q, k, q_pos, k_pos, table):
    T, S = q_pos.shape[0], k_pos.shape[0]
    s = jnp.einsum('btkgmd,bskmd->bkgmts', q, k, preferred_element_type=jnp.float32) * (A_HD ** -0.5)
    dist = q_pos[:, None] - k_pos[None, :]
    bias = jnp.transpose(_t5_bias(dist, table).reshape(T, S, A_KV_HEADS, A_GROUP, 2), (2, 3, 4, 0, 1))
    return jnp.where(dist >= 0, s + bias, NEG)


def _online_step(carry, s, v):
    m, l, acc = carry
    m_new = jnp.maximum(m, jnp.max(s, axis=-1))
    alpha = jnp.exp(m - m_new)
    p = jnp.exp(s - m_new[..., None])
    l = l * alpha + jnp.sum(p, axis=-1)
    acc = acc * alpha[..., None] + jnp.einsum('bkgmts,bskd->bkgmtd', p, v.astype(jnp.float32))
    return (m_new, l, acc)


def _diff_out(o, sub_g, lam_init, w_o):
    B, T = o.shape[:2]
    o = _rms(o, sub_g) * (1 - lam_init)
    return o.reshape(B, T, A_OUT) @ w_o


def _diff_prompt(h, w_qkv, qk_g, lam_p, sub_g, w_o, table, lam_init):
    B, S, _ = h.shape
    q, k, v = _diff_qkv(h, w_qkv, qk_g)
    lam = _diff_lambda(lam_p, lam_init)
    nqb = S // Q_BLOCK
    q_blocks = jnp.moveaxis(q.reshape(B, nqb, Q_BLOCK, A_KV_HEADS, A_GROUP, 2, A_HD), 1, 0)
    k_pos = jnp.arange(S, dtype=jnp.int32)

    def one_block(args):
        qb, i = args
        q_pos = i * Q_BLOCK + jnp.arange(Q_BLOCK, dtype=jnp.int32)
        p = jax.nn.softmax(_diff_logits(qb, k, q_pos, k_pos, table), axis=-1)
        w = p[:, :, :, 0] - lam * p[:, :, :, 1]
        return jnp.einsum('bkgts,bskd->btkgd', w.astype(v.dtype), v)

    o = lax.map(one_block, (q_blocks, jnp.arange(nqb, dtype=jnp.int32)))
    o = jnp.moveaxis(o, 0, 1).reshape(B, S, A_KV_HEADS, A_GROUP, A_VD)
    y = _diff_out(o, sub_g, lam_init, w_o)
    n_pg = S // PAGE_SIZE
    return (y, k.reshape(B, n_pg, PAGE_SIZE, A_KV_HEADS, 2 * A_HD),
            v.reshape(B, n_pg, PAGE_SIZE, A_KV_HEADS, A_VD))


def _diff_sample(h, cache_k, cache_v, layer, page_table, w_qkv, qk_g, lam_p, sub_g, w_o, table, lam_init):
    B, T, _ = h.shape
    q, k, v = _diff_qkv(h, w_qkv, qk_g)
    lam = _diff_lambda(lam_p, lam_init)
    n_pages = page_table.shape[1]
    q_pos = n_pages * PAGE_SIZE + jnp.arange(T, dtype=jnp.int32)

    def page_step(carry, xs):
        phys, j = xs
        kb = cache_k[layer, phys].reshape(B, PAGE_SIZE, A_KV_HEADS, 2, A_HD)
        vb = cache_v[layer, phys]
        k_pos = j * PAGE_SIZE + jnp.arange(PAGE_SIZE, dtype=jnp.int32)
        return _online_step(carry, _diff_logits(q, kb, q_pos, k_pos, table), vb), None

    init = (jnp.full((B, A_KV_HEADS, A_GROUP, 2, T), NEG, jnp.float32),
            jnp.zeros((B, A_KV_HEADS, A_GROUP, 2, T), jnp.float32),
            jnp.zeros((B, A_KV_HEADS, A_GROUP, 2, T, A_VD), jnp.float32))
    carry, _ = lax.scan(page_step, init, (page_table.T, jnp.arange(n_pages, dtype=jnp.int32)))
    _, l, acc = _online_step(carry, _diff_logits(q, k, q_pos, q_pos, table), v)
    o = acc / l[..., None]
    o = o[:, :, :, 0] - lam * o[:, :, :, 1]
    o = jnp.transpose(o, (0, 3, 1, 2, 4)).astype(h.dtype)
    y = _diff_out(o, sub_g, lam_init, w_o)
    return y, k.reshape(B, T, A_KV_HEADS, 2 * A_HD), v


def _swa_qkv(h, w_qkv, qk_g):
    B, T, _ = h.shape
    q, k, v = jnp.split(h @ w_qkv, [B_Q_COLS, B_Q_COLS + B_K_COLS], axis=-1)
    q = _rms(q.reshape(B, T, B_KV_HEADS, B_GROUP, B_HD), qk_g[0])
    k = _rms(k.reshape(B, T, B_KV_HEADS, B_HD), qk_g[1])
    v = v.reshape(B, T, B_KV_HEADS, B_HD)
    return q, k, v


def _sink_softmax(s, sinks):
    sk = sinks.astype(jnp.float32).reshape(B_KV_HEADS, B_GROUP)[..., None, None]
    m = jnp.maximum(jnp.max(s, axis=-1, keepdims=True), sk)
    e = jnp.exp(s - m)
    return e / (jnp.sum(e, axis=-1, keepdims=True) + jnp.exp(sk - m))


def _swa_bias(dist, table):
    Tq, Sk = dist.shape
    return jnp.transpose(_t5_bias(dist, table).reshape(Tq, Sk, B_KV_HEADS, B_GROUP), (2, 3, 0, 1))


def _swa_prompt(h, w_qkv, qk_g, sinks, w_o, table):
    B, S, _ = h.shape
    q, k, v = _swa_qkv(h, w_qkv, qk_g)
    nb = S // WINDOW
    qb = q.reshape(B, nb, WINDOW, B_KV_HEADS, B_GROUP, B_HD)
    kb = k.reshape(B, nb, WINDOW, B_KV_HEADS, B_HD)
    vb = v.reshape(B, nb, WINDOW, B_KV_HEADS, B_HD)
    prev = lambda a: jnp.concatenate([jnp.zeros_like(a[:, :1]), a[:, :-1]], axis=1)
    k_band = jnp.concatenate([prev(kb), kb], axis=2)
    v_band = jnp.concatenate([prev(vb), vb], axis=2)
    s = jnp.einsum('bnqkgd,bnskd->bnkgqs', qb, k_band, preferred_element_type=jnp.float32) * (B_HD ** -0.5)
    q_loc = jnp.arange(WINDOW, dtype=jnp.int32)
    s_loc = jnp.arange(2 * WINDOW, dtype=jnp.int32)
    dist = q_loc[:, None] + WINDOW - s_loc[None, :]
    in_window = (dist >= 0) & (dist <= WINDOW)
    blk_valid = (jnp.arange(nb)[:, None] > 0) | (s_loc[None, :] >= WINDOW)
    valid = in_window[None] & blk_valid[:, None, :]
    s = jnp.where(valid[None, :, None, None], s + _swa_bias(dist, table), NEG)
    p = _sink_softmax(s, sinks)
    o = jnp.einsum('bnkgqs,bnskd->bnqkgd', p.astype(v.dtype), v_band)
    y = o.reshape(B, S, B_OUT) @ w_o
    return y, k[:, -WINDOW:], v[:, -WINDOW:]


def _swa_sample(h, buf_k, buf_v, w_qkv, qk_g, sinks, w_o, table):
    B, T, _ = h.shape
    q, k, v = _swa_qkv(h, w_qkv, qk_g)
    kk = jnp.concatenate([buf_k, k], axis=1)
    vv = jnp.concatenate([buf_v, v], axis=1)
    q_idx = WINDOW + jnp.arange(T, dtype=jnp.int32)
    s_idx = jnp.arange(WINDOW + T, dtype=jnp.int32)
    dist = q_idx[:, None] - s_idx[None, :]
    s = jnp.einsum('btkgd,bskd->bkgts', q, kk, preferred_element_type=jnp.float32) * (B_HD ** -0.5)
    s = jnp.where((dist >= 0) & (dist <= WINDOW), s + _swa_bias(dist, table), NEG)
    p = _sink_softmax(s, sinks)
    o = jnp.einsum('bkgts,bskd->btkgd', p.astype(vv.dtype), vv)
    y = o.reshape(B, T, B_OUT) @ w_o
    return y, kk[:, -WINDOW:], vv[:, -WINDOW:]


def setup_inputs(seed: int = 0) -> dict:
    key = jax.random.key(seed)
    ks = jax.random.split(key, 32)
    f32 = jnp.float32
    n_pages = PAST_LEN // PAGE_SIZE
    n_used = DEC_BATCH * n_pages
    n_phys = n_used + max(1, n_used // 4)
    nrm = lambda k, shape, s=1.0: jax.random.normal(k, shape, f32) * s
    page_table = jax.random.permutation(ks[7], n_phys)[:n_used].reshape(DEC_BATCH, n_pages).astype(jnp.int32)
    return {
        'x_prompt': nrm(ks[0], (BATCH, SEQ, D_MODEL)),
        'x_sample': nrm(ks[1], (DEC_BATCH, DEC_SEQ, D_MODEL)),
        'cache_k_a': nrm(ks[2], (N_A_LAYERS, n_phys, PAGE_SIZE, A_KV_HEADS, 2 * A_HD)),
        'cache_v_a': nrm(ks[3], (N_A_LAYERS, n_phys, PAGE_SIZE, A_KV_HEADS, A_VD)),
        'cache_win_k': nrm(ks[4], (N_B_LAYERS, DEC_BATCH, WINDOW, B_KV_HEADS, B_HD)),
        'cache_win_v': nrm(ks[5], (N_B_LAYERS, DEC_BATCH, WINDOW, B_KV_HEADS, B_HD)),
        'page_table': page_table,
        'c_prompt': nrm(ks[8], (BATCH, D_MODEL)),
        'c_sample': nrm(ks[9], (DEC_BATCH, D_MODEL)),
        'rel_bias': nrm(ks[10], (NUM_BUCKETS, BIAS_HEADS), 0.5),
        'w_ada': nrm(ks[11], (DEPTH, D_MODEL, 6 * D_MODEL), 0.5 * D_MODEL ** -0.5),
        'b_ada': nrm(ks[12], (DEPTH, 6 * D_MODEL), 0.02),
        'g_attn': 1.0 + nrm(ks[13], (DEPTH, D_MODEL), 0.02),
        'g_ffn': 1.0 + nrm(ks[14], (DEPTH, D_MODEL), 0.02),
        'w_qkv_a': nrm(ks[15], (N_A_LAYERS, D_MODEL, A_QKV_COLS), D_MODEL ** -0.5),
        'qk_g_a': 1.0 + nrm(ks[16], (N_A_LAYERS, 2, 2, A_HD), 0.02),
        'lam_a': nrm(ks[17], (N_A_LAYERS, 4, A_HD), 0.1),
        'sub_g_a': 1.0 + nrm(ks[18], (N_A_LAYERS, A_VD), 0.02),
        'w_o_a': nrm(ks[19], (N_A_LAYERS, A_OUT, D_MODEL), A_OUT ** -0.5),
        'w_qkv_b': nrm(ks[20], (N_B_LAYERS, D_MODEL, B_QKV_COLS), D_MODEL ** -0.5),
        'qk_g_b': 1.0 + nrm(ks[21], (N_B_LAYERS, 2, B_HD), 0.02),
        'sinks_b': nrm(ks[22], (N_B_LAYERS, B_HEADS), 0.5),
        'w_o_b': nrm(ks[23], (N_B_LAYERS, B_OUT, D_MODEL), B_OUT ** -0.5),
        'w_gate_up': nrm(ks[24], (DEPTH, D_MODEL, 2 * D_FF), D_MODEL ** -0.5),
        'w_down': nrm(ks[25], (DEPTH, D_FF, D_MODEL), D_FF ** -0.5),
    }


def reference(x_prompt, x_sample, cache_k_a, cache_v_a, cache_win_k, cache_win_v, page_table,
              c_prompt, c_sample, rel_bias, w_ada, b_ada, g_attn, g_ffn,
              w_qkv_a, qk_g_a, lam_a, sub_g_a, w_o_a, w_qkv_b, qk_g_b, sinks_b, w_o_b,
              w_gate_up, w_down):
    yp, ys = x_prompt, x_sample
    kpa, vpa, ksa, vsa = [], [], [], []
    kpb, vpb, ksb, vsb = [], [], [], []
    for i in range(DEPTH):
        mp = _ada(c_prompt, w_ada[i], b_ada[i])
        ms = _ada(c_sample, w_ada[i], b_ada[i])
        hp = _modulate(yp, g_attn[i], mp[0], mp[1])
        hs = _modulate(ys, g_attn[i], ms[0], ms[1])
        if i % N_MIXERS == 0:
            ia = i // N_MIXERS
            lam_init = 0.8 - 0.6 * math.exp(-0.3 * i)
            ap, kp, vp = _diff_prompt(hp, w_qkv_a[ia], qk_g_a[ia], lam_a[ia], sub_g_a[ia], w_o_a[ia], rel_bias, lam_init)
            a_s, k_s, v_s = _diff_sample(hs, cache_k_a, cache_v_a, ia, page_table, w_qkv_a[ia], qk_g_a[ia],
                                         lam_a[ia], sub_g_a[ia], w_o_a[ia], rel_bias, lam_init)
            kpa.append(kp); vpa.append(vp); ksa.append(k_s); vsa.append(v_s)
        else:
            ib = i // N_MIXERS
            ap, kp, vp = _swa_prompt(hp, w_qkv_b[ib], qk_g_b[ib], sinks_b[ib], w_o_b[ib], rel_bias)
            a_s, k_s, v_s = _swa_sample(hs, cache_win_k[ib], cache_win_v[ib], w_qkv_b[ib], qk_g_b[ib],
                                        sinks_b[ib], w_o_b[ib], rel_bias)
            kpb.append(kp); vpb.append(vp); ksb.append(k_s); vsb.append(v_s)
        yp = yp + mp[2][:, None] * ap
        ys = ys + ms[2][:, None] * a_s
        yp = yp + mp[5][:, None] * _swiglu(_modulate(yp, g_ffn[i], mp[3], mp[4]), w_gate_up[i], w_down[i])
        ys = ys + ms[5][:, None] * _swiglu(_modulate(ys, g_ffn[i], ms[3], ms[4]), w_gate_up[i], w_down[i])
    return (yp, ys, jnp.stack(kpa), jnp.stack(vpa), jnp.stack(ksa), jnp.stack(vsa),
            jnp.stack(kpb), jnp.stack(vpb), jnp.stack(ksb), jnp.stack(vsb))
```

```python
import functools
import math

import jax
import jax.numpy as jnp
import numpy as np
from jax import lax
from jax.experimental import pallas as pl
from jax.experimental.pallas import tpu as pltpu

F32 = jnp.float32
BF16 = jnp.bfloat16

D_MODEL = 2048
DEPTH = 4
PAGE = 128
A_HEADS, A_KVH, A_HD, A_VD = 16, 8, 64, 128
A_Q = A_HEADS * 2 * A_HD
A_K = A_KVH * 2 * A_HD
A_V = A_KVH * A_VD
B_HEADS, B_KVH, B_HD = 32, 4, 64
B_GROUP = B_HEADS // B_KVH
B_Q = B_HEADS * B_HD
B_K = B_KVH * B_HD
WINDOW = 128
D_FF = 5632
NUM_BUCKETS = 32
MAX_DISTANCE = 128
MAX_EXACT = NUM_BUCKETS // 2
RMS_EPS = 1e-6
NEG = -1e30
QK_GROUP = 64

VMEM_LIMIT = 56 * 1024 * 1024


def _t5_saturation_distance():
    d = np.arange(0, 4 * MAX_DISTANCE, dtype=np.int32)
    nf = np.maximum(d, MAX_EXACT).astype(np.float32)
    large = MAX_EXACT + (np.log(nf / MAX_EXACT) / math.log(MAX_DISTANCE / MAX_EXACT)
                         * (NUM_BUCKETS - MAX_EXACT)).astype(np.int32)
    bucket = np.where(d < MAX_EXACT, d, np.minimum(large, NUM_BUCKETS - 1))
    below = np.nonzero(bucket < NUM_BUCKETS - 1)[0]
    return int(below.max()) + 1


T5_SAT_DIST = _t5_saturation_distance()
FAR_BUCKET = NUM_BUCKETS - 1


def _cparams(sem):
    return pltpu.CompilerParams(dimension_semantics=sem, vmem_limit_bytes=VMEM_LIMIT)


def _bucket(dist):
    n = jnp.maximum(dist, 0)
    nf = jnp.maximum(n, MAX_EXACT).astype(F32)
    large = MAX_EXACT + (jnp.log(nf / MAX_EXACT) / math.log(MAX_DISTANCE / MAX_EXACT)
                         * (NUM_BUCKETS - MAX_EXACT)).astype(jnp.int32)
    return jnp.where(n < MAX_EXACT, n, jnp.minimum(large, NUM_BUCKETS - 1))


def _table_lookup(bucket, tab_ref, col):
    acc = jnp.zeros(bucket.shape, F32)
    for b in range(NUM_BUCKETS):
        acc = jnp.where(bucket == b, tab_ref[b, col], acc)
    return acc


def _ada_kernel(c_ref, w_ref, b_ref, o_ref):
    c = c_ref[...]
    a = (c * jax.nn.sigmoid(c)).astype(BF16)
    o_ref[...] = jnp.dot(a, w_ref[...].astype(BF16), preferred_element_type=F32) + b_ref[...]


def _ada_all(c_all, w_ada, b_ada, tn=1024):
    mp = c_all.shape[0]
    per_chunk = D_MODEL // tn
    return pl.pallas_call(
        _ada_kernel,
        grid=(DEPTH, 6 * per_chunk),
        in_specs=[
            pl.BlockSpec((mp, D_MODEL), lambda l, j: (0, 0)),
            pl.BlockSpec((None, D_MODEL, tn), lambda l, j: (l, 0, j)),
            pl.BlockSpec((None, 1, tn), lambda l, j: (l, 0, j)),
        ],
        out_specs=pl.BlockSpec((None, None, mp, tn), lambda l, j: (l, j // per_chunk, 0, j % per_chunk)),
        out_shape=jax.ShapeDtypeStruct((DEPTH, 6, mp, D_MODEL), F32),
        compiler_params=_cparams(("arbitrary", "arbitrary")),
        name="ada_mod",
    )(c_all, w_ada, b_ada.reshape(DEPTH, 1, 6 * D_MODEL))


def _bias_near_kernel(tab_ref, o_ref, *, t):
    kind = pl.program_id(0)
    kvh = pl.program_id(1)
    m = pl.program_id(2)
    r = lax.broadcasted_iota(jnp.int32, (t, t), 0)
    c = lax.broadcasted_iota(jnp.int32, (t, t), 1)
    dist = r - c + (1 - kind) * t
    bucket = _bucket(dist)
    for g in range(2):
        col = kvh * 4 + g * 2 + m
        tile = _table_lookup(bucket, tab_ref, col)
        o_ref[g * t:(g + 1) * t, :] = jnp.where(dist >= 0, tile, NEG)


def _bias_near(rel_bias, t):
    return pl.pallas_call(
        functools.partial(_bias_near_kernel, t=t),
        grid=(2, A_KVH, 2),
        in_specs=[pl.BlockSpec(memory_space=pltpu.SMEM)],
        out_specs=pl.BlockSpec((None, None, None, 2 * t, t), lambda k, h, m: (k, h, m, 0, 0)),
        out_shape=jax.ShapeDtypeStruct((2, A_KVH, 2, 2 * t, t), F32),
        compiler_params=_cparams(("arbitrary",) * 3),
        name="bias_near",
    )(rel_bias)


def _bias_swa_kernel(tab_ref, o_ref):
    kvh = pl.program_id(0)
    q = lax.broadcasted_iota(jnp.int32, (WINDOW, 2 * WINDOW), 0)
    s = lax.broadcasted_iota(jnp.int32, (WINDOW, 2 * WINDOW), 1)
    dist = q + WINDOW - s
    bucket = _bucket(dist)
    valid = (dist >= 0) & (dist <= WINDOW)
    for g in range(B_GROUP):
        tile = _table_lookup(bucket, tab_ref, kvh * B_GROUP + g)
        o_ref[g * WINDOW:(g + 1) * WINDOW, :] = jnp.where(valid, tile, NEG)


def _bias_swa(rel_bias):
    return pl.pallas_call(
        _bias_swa_kernel,
        grid=(B_KVH,),
        in_specs=[pl.BlockSpec(memory_space=pltpu.SMEM)],
        out_specs=pl.BlockSpec((None, B_GROUP * WINDOW, 2 * WINDOW), lambda h: (h, 0, 0)),
        out_shape=jax.ShapeDtypeStruct((B_KVH, B_GROUP * WINDOW, 2 * WINDOW), F32),
        compiler_params=_cparams(("arbitrary",)),
        name="bias_swa",
    )(rel_bias)


def _bias_dec_kernel(tab_t_ref, o_ref):
    s = lax.broadcasted_iota(jnp.int32, (NUM_BUCKETS, 2 * WINDOW), 1)
    bucket = _bucket(WINDOW - s)
    tab_t = tab_t_ref[...]
    acc = jnp.zeros((NUM_BUCKETS, 2 * WINDOW), F32)
    for b in range(NUM_BUCKETS):
        acc = jnp.where(bucket == b, tab_t[:, b:b + 1], acc)
    o_ref[...] = acc


def _bias_dec(rel_bias):
    return pl.pallas_call(
        _bias_dec_kernel,
        out_shape=jax.ShapeDtypeStruct((NUM_BUCKETS, 2 * WINDOW), F32),
        name="bias_dec",
    )(rel_bias.T)


def _modulate(x, g, shift, scale):
    ms = jnp.mean(x * x, axis=-1, keepdims=True)
    h = x * lax.rsqrt(ms + RMS_EPS) * g
    return h * (1.0 + scale) + shift


def _mod_specs(mod, chunks, tm, rows_per_batch):
    specs = []
    for c in chunks:
        if mod.ndim == 4:
            tiles = rows_per_batch // tm
            specs.append(pl.BlockSpec((None, None, 1, D_MODEL),
                                      lambda i, *_, c=c, tiles=tiles: (i // tiles, c, 0, 0)))
        else:
            specs.append(pl.BlockSpec((None, tm, D_MODEL), lambda i, *_, c=c: (c, i, 0)))
    return specs


def _qkv_kernel(x_ref, g_ref, shift_ref, scale_ref, w_ref, gain_ref, nmask_ref, p_ref,
                ob_ref, k_ref, v_ref, h_ref, *, tn, f32_parts):
    j = pl.program_id(1)

    @pl.when(j == 0)
    def _():
        h_ref[...] = _modulate(x_ref[...], g_ref[...], shift_ref[...], scale_ref[...]).astype(BF16)

    acc = jnp.dot(h_ref[...], w_ref[...], preferred_element_type=F32)
    sq = (acc * acc).astype(BF16)
    pm = p_ref[...]
    pw = pm.shape[0]
    ms = jnp.concatenate(
        [jnp.dot(sq[:, c * pw:(c + 1) * pw], pm, preferred_element_type=F32) for c in range(tn // pw)], axis=1)
    nrm = acc * lax.rsqrt(ms + RMS_EPS) * gain_ref[...]
    out = jnp.where(nmask_ref[...] > 0.0, nrm, acc)
    ob_ref[...] = out.astype(BF16)
    for (jj, lo, hi, which) in f32_parts:
        dst = k_ref if which == 0 else v_ref

        @pl.when(j == jj)
        def _(dst=dst, lo=lo, hi=hi):
            dst[...] = out[:, lo:hi]


def _qkv_proj(x, g, mod, w, layer, gain, nmask, *, tm, tn, kw, rows_per_batch):
    m_rows, n = x.shape[0], w.shape[2]
    q_cols = n - 2 * kw
    if tn == kw:
        f32_parts = ((q_cols // tn, 0, kw, 0), (q_cols // tn + 1, 0, kw, 1))
    else:
        assert tn == 2 * kw
        f32_parts = ((q_cols // tn, 0, kw, 0), (q_cols // tn, kw, 2 * kw, 1))
    pw = 256
    idx = np.arange(pw)
    pmat = jnp.asarray((idx[:, None] // QK_GROUP == idx[None, :] // QK_GROUP) / QK_GROUP, BF16)
    shift_spec, scale_spec = _mod_specs(mod, (0, 1), tm, rows_per_batch)
    return pl.pallas_call(
        functools.partial(_qkv_kernel, tn=tn, f32_parts=f32_parts),
        grid=(m_rows // tm, n // tn),
        in_specs=[
            pl.BlockSpec((tm, D_MODEL), lambda i, j: (i, 0)),
            pl.BlockSpec((1, D_MODEL), lambda i, j: (0, 0)),
            shift_spec, scale_spec,
            pl.BlockSpec((None, D_MODEL, tn), lambda i, j: (layer, 0, j)),
            pl.BlockSpec((1, tn), lambda i, j: (0, j)),
            pl.BlockSpec((1, tn), lambda i, j: (0, j)),
            pl.BlockSpec((pw, pw), lambda i, j: (0, 0)),
        ],
        out_specs=[
            pl.BlockSpec((tm, tn), lambda i, j: (i, j)),
            pl.BlockSpec((tm, kw), lambda i, j: (i, 0)),
            pl.BlockSpec((tm, kw), lambda i, j: (i, 0)),
        ],
        out_shape=[
            jax.ShapeDtypeStruct((m_rows, n), BF16),
            jax.ShapeDtypeStruct((m_rows, kw), F32),
            jax.ShapeDtypeStruct((m_rows, kw), F32),
        ],
        scratch_shapes=[pltpu.VMEM((tm, D_MODEL), BF16)],
        compiler_params=_cparams(("arbitrary", "arbitrary")),
        name="qkv_proj",
    )(x, g, mod, mod, w, gain, nmask, pmat)


def _oproj_kernel(o_ref, w_ref, y_ref, gate_ref, out_ref):
    out_ref[...] = y_ref[...] + gate_ref[...] * jnp.dot(o_ref[...], w_ref[...], preferred_element_type=F32)


def _out_proj(o, w, layer, y, mod, *, tm, rows_per_batch):
    m_rows = y.shape[0]
    (gate_spec,) = _mod_specs(mod, (2,), tm, rows_per_batch)
    return pl.pallas_call(
        _oproj_kernel,
        grid=(m_rows // tm,),
        in_specs=[
            pl.BlockSpec((tm, o.shape[1]), lambda i: (i, 0)),
            pl.BlockSpec((None,) + w.shape[1:], lambda i: (layer, 0, 0)),
            pl.BlockSpec((tm, D_MODEL), lambda i: (i, 0)),
            gate_spec,
        ],
        out_specs=pl.BlockSpec((tm, D_MODEL), lambda i: (i, 0)),
        out_shape=jax.ShapeDtypeStruct((m_rows, D_MODEL), F32),
        compiler_params=_cparams(("arbitrary",)),
        name="out_proj",
    )(o, w, y, mod)


def _ffn_kernel(y_ref, g_ref, shift_ref, scale_ref, gate_ref, wg_ref, wu_ref, wd_ref, out_ref,
                h_ref, acc_ref):
    j = pl.program_id(1)

    @pl.when(j == 0)
    def _():
        h_ref[...] = _modulate(y_ref[...], g_ref[...], shift_ref[...], scale_ref[...]).astype(BF16)
        acc_ref[...] = jnp.zeros_like(acc_ref)

    h = h_ref[...]
    gate = jnp.dot(h, wg_ref[...], preferred_element_type=F32)
    up = jnp.dot(h, wu_ref[...], preferred_element_type=F32)
    act = (gate * jax.nn.sigmoid(gate) * up).astype(BF16)
    acc_ref[...] += jnp.dot(act, wd_ref[...], preferred_element_type=F32)

    @pl.when(j == pl.num_programs(1) - 1)
    def _():
        out_ref[...] = y_ref[...] + gate_ref[...] * acc_ref[...]


def _ffn(y, g, mod, w_gu, w_down, layer, *, tm, tf, rows_per_batch):
    m_rows = y.shape[0]
    nf = D_FF // tf
    shift_spec, scale_spec, gate_spec = _mod_specs(mod, (3, 4, 5), tm, rows_per_batch)
    return pl.pallas_call(
        _ffn_kernel,
        grid=(m_rows // tm, nf),
        in_specs=[
            pl.BlockSpec((tm, D_MODEL), lambda i, j: (i, 0)),
            pl.BlockSpec((1, D_MODEL), lambda i, j: (0, 0)),
            shift_spec, scale_spec, gate_spec,
            pl.BlockSpec((None, D_MODEL, tf), lambda i, j: (layer, 0, j)),
            pl.BlockSpec((None, D_MODEL, tf), lambda i, j: (layer, 0, j + nf)),
            pl.BlockSpec((None, tf, D_MODEL), lambda i, j: (layer, j, 0)),
        ],
        out_specs=pl.BlockSpec((tm, D_MODEL), lambda i, j: (i, 0)),
        out_shape=jax.ShapeDtypeStruct((m_rows, D_MODEL), F32),
        scratch_shapes=[pltpu.VMEM((tm, D_MODEL), BF16), pltpu.VMEM((tm, D_MODEL), F32)],
        compiler_params=_cparams(("arbitrary", "arbitrary")),
        name="ffn",
    )(y, g, mod, mod, mod, w_gu, w_gu, w_down)


def _softmax_step(s, v, m_ref, l_ref, acc_ref, mi):
    m_old = m_ref[mi]
    m_new = jnp.maximum(m_old, jnp.max(s, axis=-1, keepdims=True))
    p = jnp.exp(s - m_new)
    alpha = jnp.exp(m_old - m_new)
    l_ref[mi] = alpha * l_ref[mi] + jnp.sum(p, axis=-1, keepdims=True)
    acc_ref[mi] = alpha * acc_ref[mi] + jnp.dot(p.astype(BF16), v, preferred_element_type=F32)
    m_ref[mi] = m_new


def _diff_lambda(lam_ref, lam_init):
    lp = lam_ref[...]
    a = jnp.sum(lp[0:1] * lp[1:2], axis=-1, keepdims=True)
    b = jnp.sum(lp[2:3] * lp[3:4], axis=-1, keepdims=True)
    return jnp.exp(a) - jnp.exp(b) + lam_init


def _head_rms(o, subg, lam_init):
    ms = jnp.mean(o * o, axis=-1, keepdims=True)
    return o * lax.rsqrt(ms + RMS_EPS) * subg * (1.0 - lam_init)


def _diff_flash_kernel(tab_ref, lam_ref, subg_ref, q_ref, k_ref, v_ref, bias_ref, o_ref,
                       qz_ref, m_ref, l_ref, acc_ref, *, t, lam_init):
    kvh = pl.program_id(1)
    qi = pl.program_id(2)
    nt = (((1,), (1,)), ((), ()))

    q = q_ref[...]
    lane = lax.broadcasted_iota(jnp.int32, (t, 2 * A_HD), 1)
    for g in range(2):
        qg = q[:, g * 2 * A_HD:(g + 1) * 2 * A_HD]
        qz_ref[0, g * t:(g + 1) * t, :] = jnp.where(lane < A_HD, qg, jnp.zeros_like(qg))
        qz_ref[1, g * t:(g + 1) * t, :] = jnp.where(lane >= A_HD, qg, jnp.zeros_like(qg))
    m_ref[...] = jnp.full(m_ref.shape, NEG, F32)
    l_ref[...] = jnp.zeros(l_ref.shape, F32)
    acc_ref[...] = jnp.zeros(acc_ref.shape, F32)

    def far_body(ki, carry):
        start = pl.multiple_of(ki * t, t)
        k = k_ref[pl.ds(start, t), :]
        v = v_ref[pl.ds(start, t), :]
        for mi in range(2):
            s = lax.dot_general(qz_ref[mi], k, nt, preferred_element_type=F32)
            _softmax_step(s, v, m_ref, l_ref, acc_ref, mi)
        return carry

    lax.fori_loop(0, jnp.maximum(qi - 1, 0), far_body, 0)

    row = lax.broadcasted_iota(jnp.int32, (2 * t, 1), 0)
    for mi in range(2):
        c0 = tab_ref[FAR_BUCKET, kvh * 4 + mi]
        c1 = tab_ref[FAR_BUCKET, kvh * 4 + 2 + mi]
        m_ref[mi] = m_ref[mi] + jnp.where(row < t, c0, c1)

    def near_block(ki, kind):
        start = pl.multiple_of(ki * t, t)
        k = k_ref[pl.ds(start, t), :]
        v = v_ref[pl.ds(start, t), :]
        for mi in range(2):
            s = lax.dot_general(qz_ref[mi], k, nt, preferred_element_type=F32) + bias_ref[kind, mi]
            _softmax_step(s, v, m_ref, l_ref, acc_ref, mi)

    @pl.when(qi > 0)
    def _():
        near_block(qi - 1, 0)

    near_block(qi, 1)

    lam = _diff_lambda(lam_ref, lam_init)
    for g in range(2):
        rows = slice(g * t, (g + 1) * t)
        o0 = acc_ref[0, rows, :] / l_ref[0, rows, :]
        o1 = acc_ref[1, rows, :] / l_ref[1, rows, :]
        o = _head_rms(o0 - lam * o1, subg_ref[...], lam_init)
        o_ref[:, g * A_VD:(g + 1) * A_VD] = o.astype(BF16)


def _diff_flash(qkv, bias_near, rel_bias, lam_p, sub_g, *, batch, seq, t, lam_init):
    assert t + 1 >= T5_SAT_DIST
    nq = seq // t
    kcol0 = A_Q // (2 * A_HD)
    vcol0 = (A_Q + A_K) // A_VD
    return pl.pallas_call(
        functools.partial(_diff_flash_kernel, t=t, lam_init=lam_init),
        grid=(batch, A_KVH, nq),
        in_specs=[
            pl.BlockSpec(memory_space=pltpu.SMEM),
            pl.BlockSpec((4, A_HD), lambda b, h, i: (0, 0)),
            pl.BlockSpec((1, A_VD), lambda b, h, i: (0, 0)),
            pl.BlockSpec((t, 2 * A_VD), lambda b, h, i, nq=nq: (b * nq + i, h)),
            pl.BlockSpec((seq, 2 * A_HD), lambda b, h, i: (b, kcol0 + h)),
            pl.BlockSpec((seq, A_VD), lambda b, h, i: (b, vcol0 + h)),
            pl.BlockSpec((2, None, 2, 2 * t, t), lambda b, h, i: (0, h, 0, 0, 0)),
        ],
        out_specs=pl.BlockSpec((t, 2 * A_VD), lambda b, h, i, nq=nq: (b * nq + i, h)),
        out_shape=jax.ShapeDtypeStruct((batch * seq, A_HEADS * A_VD), BF16),
        scratch_shapes=[
            pltpu.VMEM((2, 2 * t, 2 * A_HD), BF16),
            pltpu.VMEM((2, 2 * t, 1), F32),
            pltpu.VMEM((2, 2 * t, 1), F32),
            pltpu.VMEM((2, 2 * t, A_VD), F32),
        ],
        compiler_params=_cparams(("arbitrary",) * 3),
        name="diff_flash",
    )(rel_bias, lam_p, sub_g.reshape(1, A_VD), qkv, qkv, qkv, bias_near)


def _swa_kernel(sink_ref, q_ref, kp_ref, kc_ref, vp_ref, vc_ref, bias_ref, o_ref):
    n = pl.program_id(1)
    w = WINDOW
    nt = (((1,), (1,)), ((), ()))
    q = q_ref[...]
    lane = lax.broadcasted_iota(jnp.int32, (w, 2 * B_HD), 1)
    key = lax.broadcasted_iota(jnp.int32, (1, 2 * w), 1)
    key_ok = key >= jnp.where(n > 0, 0, w)
    grp = lax.broadcasted_iota(jnp.int32, (B_GROUP * w, 1), 0) // w
    k_band = jnp.concatenate([kp_ref[...], kc_ref[...]], axis=0)
    v_band = jnp.concatenate([vp_ref[...], vc_ref[...]], axis=0)
    for kvh in range(B_KVH):
        cols = slice(kvh * B_HD, (kvh + 1) * B_HD)
        kk = k_band[:, cols]
        vv = v_band[:, cols]
        kk = jnp.concatenate([kk, kk], axis=1)
        vv = jnp.concatenate([vv, vv], axis=1)
        parts = []
        for j in range(B_GROUP // 2):
            chunk = q[:, (kvh * B_GROUP // 2 + j) * 2 * B_HD:(kvh * B_GROUP // 2 + j + 1) * 2 * B_HD]
            parts.append(jnp.where(lane < B_HD, chunk, jnp.zeros_like(chunk)))
            parts.append(jnp.where(lane >= B_HD, chunk, jnp.zeros_like(chunk)))
        qs = jnp.concatenate(parts, axis=0)
        s = lax.dot_general(qs, kk, nt, preferred_element_type=F32) + bias_ref[kvh]
        s = jnp.where(key_ok, s, NEG)
        sk = jnp.zeros((B_GROUP * w, 1), F32)
        for g in range(B_GROUP):
            sk = jnp.where(grp == g, sink_ref[kvh * B_GROUP + g], sk)
        mx = jnp.maximum(jnp.max(s, axis=-1, keepdims=True), sk)
        e = jnp.exp(s - mx)
        den = jnp.sum(e, axis=-1, keepdims=True) + jnp.exp(sk - mx)
        p = (e / den).astype(BF16)
        o = jnp.dot(p, vv, preferred_element_type=F32)
        for j in range(B_GROUP // 2):
            oa = o[(2 * j) * w:(2 * j + 1) * w, :]
            ob = o[(2 * j + 1) * w:(2 * j + 2) * w, :]
            c0 = (kvh * B_GROUP // 2 + j) * 2 * B_HD
            o_ref[:, c0:c0 + 2 * B_HD] = jnp.where(lane < B_HD, oa, ob).astype(BF16)


def _swa_prompt(qkv, bias_swa, sinks, *, batch, seq):
    nb = seq // WINDOW
    kblk = B_Q // B_K
    cur = lambda b, n: b * nb + n
    prev = lambda b, n: b * nb + jnp.maximum(n - 1, 0)
    return pl.pallas_call(
        _swa_kernel,
        grid=(batch, nb),
        in_specs=[
            pl.BlockSpec(memory_space=pltpu.SMEM),
            pl.BlockSpec((WINDOW, B_Q), lambda b, n: (cur(b, n), 0)),
            pl.BlockSpec((WINDOW, B_K), lambda b, n: (prev(b, n), kblk)),
            pl.BlockSpec((WINDOW, B_K), lambda b, n: (cur(b, n), kblk)),
            pl.BlockSpec((WINDOW, B_K), lambda b, n: (prev(b, n), kblk + 1)),
            pl.BlockSpec((WINDOW, B_K), lambda b, n: (cur(b, n), kblk + 1)),
            pl.BlockSpec(bias_swa.shape, lambda b, n: (0, 0, 0)),
        ],
        out_specs=pl.BlockSpec((WINDOW, B_Q), lambda b, n: (cur(b, n), 0)),
        out_shape=jax.ShapeDtypeStruct((batch * seq, B_Q), BF16),
        compiler_params=_cparams(("arbitrary", "arbitrary")),
        name="swa_prompt",
    )(sinks, qkv, qkv, qkv, qkv, qkv, bias_swa)


PAGES_PER_STEP = 8


def _paged_kernel(pt_ref, q_ref, knew_ref, vnew_ref, bias_far_ref, bias_last_ref, bias_self_ref, lam_ref, subg_ref,
                  *rest, lam_init):
    del pt_ref
    npg = PAGES_PER_STEP
    k_refs, v_refs = rest[:npg], rest[npg:2 * npg]
    o_ref, qb_ref, m_ref, l_ref, acc_ref = rest[2 * npg:]
    jg = pl.program_id(1)
    last = pl.num_programs(1) - 1
    nt = (((1,), (1,)), ((), ()))
    rows = 2 * A_HEADS
    pr = PAGE * A_KVH

    @pl.when(jg == 0)
    def _():
        q = q_ref[...]
        lane = lax.broadcasted_iota(jnp.int32, q.shape, 1)
        qb_ref[...] = jnp.concatenate([jnp.where(lane < A_HD, q, jnp.zeros_like(q)),
                                       jnp.where(lane >= A_HD, q, jnp.zeros_like(q))], axis=0)
        m_ref[...] = jnp.full(m_ref.shape, NEG, F32)
        l_ref[...] = jnp.zeros(l_ref.shape, F32)
        acc_ref[...] = jnp.zeros(acc_ref.shape, F32)

    qb = qb_ref[...]
    bias_far = bias_far_ref[...]
    bias_end = jnp.where(jg == last, bias_last_ref[...], bias_far)
    s = jnp.concatenate(
        [lax.dot_general(qb, k_refs[p][...].astype(BF16), nt, preferred_element_type=F32)
         + (bias_end if p == npg - 1 else bias_far) for p in range(npg)], axis=1)
    m_old = m_ref[...]
    m_new = jnp.maximum(m_old, jnp.max(s, axis=-1, keepdims=True))
    pf = jnp.exp(s - m_new)
    alpha = jnp.exp(m_old - m_new)
    l_ref[...] = alpha * l_ref[...] + jnp.sum(pf, axis=-1, keepdims=True)
    p = pf.astype(BF16)
    pv = jnp.dot(p[:, 0:pr], v_refs[0][...].astype(BF16), preferred_element_type=F32)
    for i in range(1, npg):
        pv = pv + jnp.dot(p[:, i * pr:(i + 1) * pr], v_refs[i][...].astype(BF16), preferred_element_type=F32)
    acc_ref[...] = alpha * acc_ref[...] + pv
    m_ref[...] = m_new

    @pl.when(jg == last)
    def _():
        s_self = jnp.sum(qb.astype(F32) * knew_ref[...], axis=-1, keepdims=True) + bias_self_ref[...]
        m_o = m_ref[...]
        m_n = jnp.maximum(m_o, s_self)
        p_self = jnp.exp(s_self - m_n)
        a = jnp.exp(m_o - m_n)
        l = a * l_ref[...] + p_self
        o = (a * acc_ref[...] + p_self * vnew_ref[...]) / l
        od = o[0:A_HEADS, :] - _diff_lambda(lam_ref, lam_init) * o[A_HEADS:rows, :]
        o_ref[...] = _head_rms(od, subg_ref[...], lam_init)


def _paged_decode(q, knew_rows, vnew_rows, cache_k, cache_v, layer, page_table, bias_far, bias_last, bias_self,
                  lam_p, sub_g, *, lam_init):
    nb, n_pages = page_table.shape
    npg = PAGES_PER_STEP
    rows = 2 * A_HEADS
    assert PAGE * (n_pages - 1) + 1 >= T5_SAT_DIST and PAGE == WINDOW
    pr = PAGE * A_KVH
    ck = cache_k.reshape(cache_k.shape[0], cache_k.shape[1], pr, 2 * A_HD)
    cv = cache_v.reshape(cache_v.shape[0], cache_v.shape[1], pr, A_VD)

    def page_spec(p):
        return pl.BlockSpec((None, None, pr, A_VD),
                            lambda b, j, pt, p=p: (layer, pt[b, j * npg + p], 0, 0))

    const2 = lambda b, j, pt: (0, 0)
    per_b = lambda b, j, pt: (b, 0, 0)
    grid_spec = pltpu.PrefetchScalarGridSpec(
        num_scalar_prefetch=1,
        grid=(nb, n_pages // npg),
        in_specs=[
            pl.BlockSpec((None, A_HEADS, 2 * A_HD), per_b),
            pl.BlockSpec((None, rows, 2 * A_HD), per_b),
            pl.BlockSpec((None, rows, A_VD), per_b),
            pl.BlockSpec((rows, pr), const2),
            pl.BlockSpec((rows, pr), const2),
            pl.BlockSpec((rows, 1), const2),
            pl.BlockSpec((4, A_HD), const2),
            pl.BlockSpec((1, A_VD), const2),
        ] + [page_spec(p) for p in range(npg)] * 2,
        out_specs=pl.BlockSpec((None, A_HEADS, A_VD), per_b),
        scratch_shapes=[
            pltpu.VMEM((rows, 2 * A_HD), BF16),
            pltpu.VMEM((rows, 1), F32),
            pltpu.VMEM((rows, 1), F32),
            pltpu.VMEM((rows, A_VD), F32),
        ],
    )
    return pl.pallas_call(
        functools.partial(_paged_kernel, lam_init=lam_init),
        grid_spec=grid_spec,
        out_shape=jax.ShapeDtypeStruct((nb, A_HEADS, A_VD), F32),
        compiler_params=_cparams(("arbitrary", "arbitrary")),
        name="paged_decode",
    )(page_table, q, knew_rows, vnew_rows, bias_far, bias_last, bias_self, lam_p, sub_g.reshape(1, A_VD),
      *([ck] * npg), *([cv] * npg))


def _swa_dec_kernel(q_ref, kbuf_ref, vbuf_ref, knew_ref, vnew_ref, bias_ref, sink_ref, o_ref):
    nt = (((1,), (1,)), ((), ()))
    q = q_ref[...]
    kbuf = kbuf_ref[...].astype(BF16)
    vbuf = vbuf_ref[...].astype(BF16)
    knew = knew_ref[...]
    vnew = vnew_ref[...]
    bias = bias_ref[...]
    sinks = sink_ref[...]
    for kvh in range(B_KVH):
        rows = slice(kvh * B_GROUP, (kvh + 1) * B_GROUP)
        cols = slice(kvh * B_HD, (kvh + 1) * B_HD)
        qh = q[rows, :]
        s = lax.dot_general(qh, kbuf[:, cols], nt, preferred_element_type=F32) + bias[rows, 0:WINDOW]
        s_new = (jnp.sum(qh.astype(F32) * knew[:, cols], axis=-1, keepdims=True)
                 + bias[rows, WINDOW:WINDOW + 1])
        sk = sinks[rows, :]
        mx = jnp.maximum(jnp.maximum(jnp.max(s, axis=-1, keepdims=True), s_new), sk)
        e = jnp.exp(s - mx)
        e_new = jnp.exp(s_new - mx)
        den = jnp.sum(e, axis=-1, keepdims=True) + e_new + jnp.exp(sk - mx)
        o = jnp.dot(e.astype(BF16), vbuf[:, cols], preferred_element_type=F32)
        o_ref[rows, :] = (o + e_new * vnew[:, cols]) / den


def _swa_decode(q, kbuf, vbuf, knew, vnew, bias_dec, sinks):
    nb = q.shape[0]
    per_b = lambda b: (b, 0, 0)
    return pl.pallas_call(
        _swa_dec_kernel,
        grid=(nb,),
        in_specs=[
            pl.BlockSpec((None, B_HEADS, B_HD), per_b),
            pl.BlockSpec((None, WINDOW, B_K), per_b),
            pl.BlockSpec((None, WINDOW, B_K), per_b),
            pl.BlockSpec((None, 1, B_K), per_b),
            pl.BlockSpec((None, 1, B_K), per_b),
            pl.BlockSpec(bias_dec.shape, lambda b: (0, 0)),
            pl.BlockSpec((B_HEADS, 1), lambda b: (0, 0)),
        ],
        out_specs=pl.BlockSpec((None, B_HEADS, B_HD), per_b),
        out_shape=jax.ShapeDtypeStruct((nb, B_HEADS, B_HD), F32),
        compiler_params=_cparams(("arbitrary",)),
        name="swa_decode",
    )(q, kbuf, vbuf, knew, vnew, bias_dec, sinks.reshape(B_HEADS, 1))


def _qk_gain_rows(qk_g, q_cols, kw, scale):
    gq = jnp.tile(qk_g[0].reshape(-1), q_cols // qk_g[0].size) * scale
    gk = jnp.tile(qk_g[1].reshape(-1), kw // qk_g[1].size)
    gain = jnp.concatenate([gq, gk, jnp.ones((kw,), F32)]).reshape(1, -1)
    nmask = jnp.concatenate([jnp.ones((q_cols + kw,), F32), jnp.zeros((kw,), F32)]).reshape(1, -1)
    return gain, nmask


def kernel(x_prompt, x_sample, cache_k_a, cache_v_a, cache_win_k, cache_win_v, page_table, c_prompt, c_sample,
           rel_bias, w_ada, b_ada, g_attn, g_ffn, w_qkv_a, qk_g_a, lam_a, sub_g_a, w_o_a, w_qkv_b, qk_g_b,
           sinks_b, w_o_b, w_gate_up, w_down):
    batch, seq, _ = x_prompt.shape
    nb = x_sample.shape[0]
    n_pages = page_table.shape[1]
    t_attn = 256
    tm_p, tm_ffn, tf = 512, 512, 512

    n_c = nb + batch
    pad = (-n_c) % 8
    c_all = jnp.concatenate([c_sample, c_prompt, jnp.zeros((pad, D_MODEL), F32)], axis=0)
    mod = _ada_all(c_all, w_ada, b_ada)

    bias_near = _bias_near(rel_bias, t_attn)
    bias_swa = _bias_swa(rel_bias)
    bias_dec = _bias_dec(rel_bias)
    bias_dec_rows = bias_dec.reshape(A_HEADS, 2, 2 * WINDOW).transpose(1, 0, 2).reshape(2 * A_HEADS, 2 * WINDOW)
    row_kvh = (jnp.arange(2 * A_HEADS) % A_HEADS) // (A_HEADS // A_KVH)
    own = row_kvh[:, None] == (jnp.arange(PAGE * A_KVH) % A_KVH)[None, :]
    pb_far = jnp.where(own, bias_dec_rows[:, 0:1], NEG)
    pb_last = jnp.where(own, jnp.repeat(bias_dec_rows[:, 0:WINDOW], A_KVH, axis=1), NEG)
    pb_self = bias_dec_rows[:, WINDOW:WINDOW + 1]

    yp = x_prompt.reshape(batch * seq, D_MODEL)
    ys = x_sample.reshape(nb, D_MODEL)
    outs = {k: [] for k in ("kpa", "vpa", "ksa", "vsa", "kpb", "vpb", "ksb", "vsb")}

    w_qkv_a16, w_o_a16 = w_qkv_a.astype(BF16), w_o_a.astype(BF16)
    w_qkv_b16, w_o_b16 = w_qkv_b.astype(BF16), w_o_b.astype(BF16)
    w_gu16, w_dn16 = w_gate_up.astype(BF16), w_down.astype(BF16)

    for i in range(DEPTH):
        mod_p = mod[i, :, nb:nb + batch].transpose(1, 0, 2).reshape(batch, 6, 1, D_MODEL)
        mod_s = mod[i, :, :nb]
        g_a = g_attn[i].reshape(1, D_MODEL)
        g_f = g_ffn[i].reshape(1, D_MODEL)
        li = i // 2
        if i % 2 == 0:
            lam_init = 0.8 - 0.6 * math.exp(-0.3 * i)
            w_qkv, w_o = w_qkv_a16, w_o_a16
            gain, nmask = _qk_gain_rows(qk_g_a[li], A_Q, A_K, A_HD ** -0.5)
            qkv_p, k_p, v_p = _qkv_proj(yp, g_a, mod_p, w_qkv, li, gain, nmask, tm=tm_p, tn=A_K, kw=A_K,
                                        rows_per_batch=seq)
            o_p = _diff_flash(qkv_p, bias_near, rel_bias, lam_a[li], sub_g_a[li], batch=batch, seq=seq,
                              t=t_attn, lam_init=lam_init)
            qkv_s, k_s, v_s = _qkv_proj(ys, g_a, mod_s, w_qkv, li, gain, nmask, tm=nb, tn=A_K, kw=A_K,
                                        rows_per_batch=None)
            kv_rows = lambda a: jnp.tile(jnp.repeat(a.reshape(nb, A_KVH, A_VD), A_HEADS // A_KVH, axis=1), (1, 2, 1))
            o_s = _paged_decode(qkv_s[:, :A_Q].reshape(nb, A_HEADS, 2 * A_HD), kv_rows(k_s), kv_rows(v_s),
                                cache_k_a, cache_v_a, li, page_table, pb_far, pb_last, pb_self,
                                lam_a[li], sub_g_a[li], lam_init=lam_init)
            o_s = o_s.reshape(nb, A_HEADS * A_VD).astype(BF16)
            outs["kpa"].append(k_p.reshape(batch, seq // PAGE, PAGE, A_KVH, 2 * A_HD))
            outs["vpa"].append(v_p.reshape(batch, seq // PAGE, PAGE, A_KVH, A_VD))
            outs["ksa"].append(k_s.reshape(nb, 1, A_KVH, 2 * A_HD))
            outs["vsa"].append(v_s.reshape(nb, 1, A_KVH, A_VD))
        else:
            w_qkv, w_o = w_qkv_b16, w_o_b16
            gain, nmask = _qk_gain_rows(qk_g_b[li], B_Q, B_K, B_HD ** -0.5)
            qkv_p, k_p, v_p = _qkv_proj(yp, g_a, mod_p, w_qkv, li, gain, nmask, tm=tm_p, tn=2 * B_K, kw=B_K,
                                        rows_per_batch=seq)
            o_p = _swa_prompt(qkv_p, bias_swa, sinks_b[li], batch=batch, seq=seq)
            qkv_s, k_s, v_s = _qkv_proj(ys, g_a, mod_s, w_qkv, li, gain, nmask, tm=nb, tn=2 * B_K, kw=B_K,
                                        rows_per_batch=None)
            buf_k = cache_win_k[li].reshape(nb, WINDOW, B_K)
            buf_v = cache_win_v[li].reshape(nb, WINDOW, B_K)
            o_s = _swa_decode(qkv_s[:, :B_Q].reshape(nb, B_HEADS, B_HD), buf_k, buf_v, k_s.reshape(nb, 1, B_K),
                              v_s.reshape(nb, 1, B_K), bias_dec, sinks_b[li])
            o_s = o_s.reshape(nb, B_Q).astype(BF16)
            last_w = lambda a: a.reshape(batch, seq, B_K)[:, seq - WINDOW:].reshape(batch, WINDOW, B_KVH, B_HD)
            outs["kpb"].append(last_w(k_p))
            outs["vpb"].append(last_w(v_p))
            outs["ksb"].append(jnp.concatenate([buf_k[:, 1:], k_s.reshape(nb, 1, B_K)], axis=1)
                               .reshape(nb, WINDOW, B_KVH, B_HD))
            outs["vsb"].append(jnp.concatenate([buf_v[:, 1:], v_s.reshape(nb, 1, B_K)], axis=1)
                               .reshape(nb, WINDOW, B_KVH, B_HD))
        yp = _out_proj(o_p, w_o, li, yp, mod_p, tm=tm_p, rows_per_batch=seq)
        ys = _out_proj(o_s, w_o, li, ys, mod_s, tm=nb, rows_per_batch=None)
        yp = _ffn(yp, g_f, mod_p, w_gu16, w_dn16, i, tm=tm_ffn, tf=tf, rows_per_batch=seq)
        ys = _ffn(ys, g_f, mod_s, w_gu16, w_dn16, i, tm=nb, tf=tf, rows_per_batch=None)

    return (yp.reshape(batch, seq, D_MODEL), ys.reshape(nb, 1, D_MODEL),
            jnp.stack(outs["kpa"]), jnp.stack(outs["vpa"]), jnp.stack(outs["ksa"]), jnp.stack(outs["vsa"]),
            jnp.stack(outs["kpb"]), jnp.stack(outs["vpb"]), jnp.stack(outs["ksb"]), jnp.stack(outs["vsb"]))
```

```python
import functools
import math

import jax
import jax.numpy as jnp
import numpy as np
from jax import lax
from jax.experimental import pallas as pl
from jax.experimental.pallas import tpu as pltpu

F32 = jnp.float32
BF16 = jnp.bfloat16

D_MODEL = 2048
DEPTH = 4
PAGE = 128
A_HEADS, A_KVH, A_HD, A_VD = 16, 8, 64, 128
A_Q = A_HEADS * 2 * A_HD
A_K = A_KVH * 2 * A_HD
A_V = A_KVH * A_VD
B_HEADS, B_KVH, B_HD = 32, 4, 64
B_GROUP = B_HEADS // B_KVH
B_Q = B_HEADS * B_HD
B_K = B_KVH * B_HD
WINDOW = 128
D_FF = 5632
NUM_BUCKETS = 32
MAX_DISTANCE = 128
MAX_EXACT = NUM_BUCKETS // 2
RMS_EPS = 1e-6
LOG2E = math.log2(math.e)
VT_ROWS = A_VD + 16
NEG = -1e30
QK_GROUP = 64

VMEM_LIMIT = 56 * 1024 * 1024


def _t5_saturation_distance():
    d = np.arange(0, 4 * MAX_DISTANCE, dtype=np.int32)
    nf = np.maximum(d, MAX_EXACT).astype(np.float32)
    large = MAX_EXACT + (np.log(nf / MAX_EXACT) / math.log(MAX_DISTANCE / MAX_EXACT)
                         * (NUM_BUCKETS - MAX_EXACT)).astype(np.int32)
    bucket = np.where(d < MAX_EXACT, d, np.minimum(large, NUM_BUCKETS - 1))
    below = np.nonzero(bucket < NUM_BUCKETS - 1)[0]
    return int(below.max()) + 1


T5_SAT_DIST = _t5_saturation_distance()
FAR_BUCKET = NUM_BUCKETS - 1


def _cparams(sem):
    return pltpu.CompilerParams(dimension_semantics=sem, vmem_limit_bytes=VMEM_LIMIT)


def _bucket(dist):
    n = jnp.maximum(dist, 0)
    nf = jnp.maximum(n, MAX_EXACT).astype(F32)
    large = MAX_EXACT + (jnp.log(nf / MAX_EXACT) / math.log(MAX_DISTANCE / MAX_EXACT)
                         * (NUM_BUCKETS - MAX_EXACT)).astype(jnp.int32)
    return jnp.where(n < MAX_EXACT, n, jnp.minimum(large, NUM_BUCKETS - 1))


def _table_lookup(bucket, tab_ref, col):
    acc = jnp.zeros(bucket.shape, F32)
    for b in range(NUM_BUCKETS):
        acc = jnp.where(bucket == b, tab_ref[b, col], acc)
    return acc


def _ada_kernel(c_ref, w_ref, b_ref, o_ref):
    c = c_ref[...]
    a = (c * jax.nn.sigmoid(c)).astype(BF16)
    o_ref[...] = jnp.dot(a, w_ref[...].astype(BF16), preferred_element_type=F32) + b_ref[...]


def _ada_all(c_all, w_ada, b_ada, tn=1024):
    mp = c_all.shape[0]
    per_chunk = D_MODEL // tn
    return pl.pallas_call(
        _ada_kernel,
        grid=(DEPTH, 6 * per_chunk),
        in_specs=[
            pl.BlockSpec((mp, D_MODEL), lambda l, j: (0, 0)),
            pl.BlockSpec((None, D_MODEL, tn), lambda l, j: (l, 0, j)),
            pl.BlockSpec((None, 1, tn), lambda l, j: (l, 0, j)),
        ],
        out_specs=pl.BlockSpec((None, None, mp, tn), lambda l, j: (l, j // per_chunk, 0, j % per_chunk)),
        out_shape=jax.ShapeDtypeStruct((DEPTH, 6, mp, D_MODEL), F32),
        compiler_params=_cparams(("arbitrary", "arbitrary")),
        name="ada_mod",
    )(c_all, w_ada, b_ada.reshape(DEPTH, 1, 6 * D_MODEL))


def _bias_near_kernel(tab_ref, o_ref, *, t):
    kind = pl.program_id(0)
    kvh = pl.program_id(1)
    m = pl.program_id(2)
    key = lax.broadcasted_iota(jnp.int32, (t, t), 0)
    qry = lax.broadcasted_iota(jnp.int32, (t, t), 1)
    dist = qry - key + (1 - kind) * t
    bucket = _bucket(dist)
    for g in range(2):
        col = kvh * 4 + g * 2 + m
        tile = _table_lookup(bucket, tab_ref, col)
        o_ref[:, g * t:(g + 1) * t] = jnp.where(dist >= 0, tile * LOG2E, NEG)


def _bias_near(rel_bias, t):
    return pl.pallas_call(
        functools.partial(_bias_near_kernel, t=t),
        grid=(2, A_KVH, 2),
        in_specs=[pl.BlockSpec(memory_space=pltpu.SMEM)],
        out_specs=pl.BlockSpec((None, None, None, t, 2 * t), lambda k, h, m: (k, h, m, 0, 0)),
        out_shape=jax.ShapeDtypeStruct((2, A_KVH, 2, t, 2 * t), F32),
        compiler_params=_cparams(("arbitrary",) * 3),
        name="bias_near",
    )(rel_bias)


def _bias_swa_kernel(tab_ref, o_ref):
    kvh = pl.program_id(0)
    q = lax.broadcasted_iota(jnp.int32, (WINDOW, 2 * WINDOW), 0)
    s = lax.broadcasted_iota(jnp.int32, (WINDOW, 2 * WINDOW), 1)
    dist = q + WINDOW - s
    bucket = _bucket(dist)
    valid = (dist >= 0) & (dist <= WINDOW)
    for g in range(B_GROUP):
        tile = _table_lookup(bucket, tab_ref, kvh * B_GROUP + g)
        o_ref[g * WINDOW:(g + 1) * WINDOW, :] = jnp.where(valid, tile, NEG)


def _bias_swa(rel_bias):
    return pl.pallas_call(
        _bias_swa_kernel,
        grid=(B_KVH,),
        in_specs=[pl.BlockSpec(memory_space=pltpu.SMEM)],
        out_specs=pl.BlockSpec((None, B_GROUP * WINDOW, 2 * WINDOW), lambda h: (h, 0, 0)),
        out_shape=jax.ShapeDtypeStruct((B_KVH, B_GROUP * WINDOW, 2 * WINDOW), F32),
        compiler_params=_cparams(("arbitrary",)),
        name="bias_swa",
    )(rel_bias)


def _bias_dec_kernel(tab_t_ref, o_ref):
    s = lax.broadcasted_iota(jnp.int32, (NUM_BUCKETS, 2 * WINDOW), 1)
    bucket = _bucket(WINDOW - s)
    tab_t = tab_t_ref[...]
    acc = jnp.zeros((NUM_BUCKETS, 2 * WINDOW), F32)
    for b in range(NUM_BUCKETS):
        acc = jnp.where(bucket == b, tab_t[:, b:b + 1], acc)
    o_ref[...] = acc


def _bias_dec(rel_bias):
    return pl.pallas_call(
        _bias_dec_kernel,
        out_shape=jax.ShapeDtypeStruct((NUM_BUCKETS, 2 * WINDOW), F32),
        name="bias_dec",
    )(rel_bias.T)


def _modulate(x, g, shift, scale):
    ms = jnp.mean(x * x, axis=-1, keepdims=True)
    h = x * lax.rsqrt(ms + RMS_EPS) * g
    return h * (1.0 + scale) + shift


def _mod_specs(mod, chunks, tm, rows_per_batch):
    specs = []
    for c in chunks:
        if mod.ndim == 4:
            tiles = rows_per_batch // tm
            specs.append(pl.BlockSpec((None, None, 1, D_MODEL),
                                      lambda i, *_, c=c, tiles=tiles: (i // tiles, c, 0, 0)))
        else:
            specs.append(pl.BlockSpec((None, tm, D_MODEL), lambda i, *_, c=c: (c, i, 0)))
    return specs


def _qkv_kernel(x_ref, g_ref, shift_ref, scale_ref, w_ref, gain_ref, p_ref, ob_ref, k_ref, v_ref, *rest,
                tn, kw, n_q_tiles, n_tiles, emit_vt):
    vt_ref = rest[0] if emit_vt else None
    h_ref = rest[-1]
    j = pl.program_id(1)
    pw = p_ref.shape[0]

    @pl.when(j == 0)
    def _():
        h_ref[...] = _modulate(x_ref[...], g_ref[...], shift_ref[...], scale_ref[...]).astype(BF16)

    def tile(parts):
        h = h_ref[...]
        proj = lambda c: jnp.dot(h, w_ref[:, c * pw:(c + 1) * pw], preferred_element_type=F32)
        nxt = proj(0)
        for c, (kind, off) in enumerate(parts):
            cols = slice(c * pw, (c + 1) * pw)
            out = nxt
            if c + 1 < len(parts):
                nxt = proj(c + 1)
            if kind != "v":
                ms = jnp.dot((out * out).astype(BF16), p_ref[...], preferred_element_type=F32)
                out = out * lax.rsqrt(ms + RMS_EPS) * gain_ref[:, cols]
            ob_ref[:, cols] = out.astype(BF16)
            if kind == "k":
                k_ref[:, off:off + pw] = out
            if kind == "v":
                v_ref[:, off:off + pw] = out
                if emit_vt:
                    vt = out.T.astype(BF16)
                    for i in range(pw // A_VD):
                        head = off // A_VD + i
                        vt_ref[head, 0:A_VD, :] = vt[i * A_VD:(i + 1) * A_VD, :]
                        vt_ref[head, A_VD:VT_ROWS, :] = jnp.ones((VT_ROWS - A_VD, vt.shape[1]), BF16)

    @pl.when(j < n_q_tiles)
    def _():
        tile([("q", 0)] * (tn // pw))

    for jj in range(n_q_tiles, n_tiles):
        offs = [(jj - n_q_tiles) * tn + c * pw for c in range(tn // pw)]
        parts = [("k", o) if o < kw else ("v", o - kw) for o in offs]
        pl.when(j == jj)(functools.partial(tile, parts))


def _qkv_proj(x, g, mod, w, layer, gain, *, tm, tn, kw, rows_per_batch, emit_vt=False):
    m_rows, n = x.shape[0], w.shape[2]
    q_cols = n - 2 * kw
    pw = 256
    assert q_cols % tn == 0 and tn % pw == 0 and kw % pw == 0
    idx = np.arange(pw)
    pmat = jnp.asarray((idx[:, None] // QK_GROUP == idx[None, :] // QK_GROUP) / QK_GROUP, BF16)
    shift_spec, scale_spec = _mod_specs(mod, (0, 1), tm, rows_per_batch)
    return pl.pallas_call(
        functools.partial(_qkv_kernel, tn=tn, kw=kw, n_q_tiles=q_cols // tn, n_tiles=n // tn, emit_vt=emit_vt),
        grid=(m_rows // tm, n // tn),
        in_specs=[
            pl.BlockSpec((tm, D_MODEL), lambda i, j: (i, 0)),
            pl.BlockSpec((1, D_MODEL), lambda i, j: (0, 0)),
            shift_spec, scale_spec,
            pl.BlockSpec((None, D_MODEL, tn), lambda i, j: (layer, 0, j)),
            pl.BlockSpec((1, tn), lambda i, j: (0, j)),
            pl.BlockSpec((pw, pw), lambda i, j: (0, 0)),
        ],
        out_specs=[
            pl.BlockSpec((tm, tn), lambda i, j: (i, j)),
            pl.BlockSpec((tm, kw), lambda i, j: (i, 0)),
            pl.BlockSpec((tm, kw), lambda i, j: (i, 0)),
        ] + ([pl.BlockSpec((A_KVH, VT_ROWS, tm), lambda i, j: (0, 0, i))] if emit_vt else []),
        out_shape=[
            jax.ShapeDtypeStruct((m_rows, n), BF16),
            jax.ShapeDtypeStruct((m_rows, kw), F32),
            jax.ShapeDtypeStruct((m_rows, kw), F32),
        ] + ([jax.ShapeDtypeStruct((A_KVH, VT_ROWS, m_rows), BF16)] if emit_vt else []),
        scratch_shapes=[pltpu.VMEM((tm, D_MODEL), BF16)],
        compiler_params=_cparams(("arbitrary", "arbitrary")),
        name="qkv_proj",
    )(x, g, mod, mod, w, gain, pmat)


def _oproj_kernel(o_ref, w_ref, y_ref, gate_ref, out_ref):
    out_ref[...] = y_ref[...] + gate_ref[...] * jnp.dot(o_ref[...], w_ref[...], preferred_element_type=F32)


def _out_proj(o, w, layer, y, mod, *, tm, rows_per_batch):
    m_rows = y.shape[0]
    (gate_spec,) = _mod_specs(mod, (2,), tm, rows_per_batch)
    return pl.pallas_call(
        _oproj_kernel,
        grid=(m_rows // tm,),
        in_specs=[
            pl.BlockSpec((tm, o.shape[1]), lambda i: (i, 0)),
            pl.BlockSpec((None,) + w.shape[1:], lambda i: (layer, 0, 0)),
            pl.BlockSpec((tm, D_MODEL), lambda i: (i, 0)),
            gate_spec,
        ],
        out_specs=pl.BlockSpec((tm, D_MODEL), lambda i: (i, 0)),
        out_shape=jax.ShapeDtypeStruct((m_rows, D_MODEL), F32),
        compiler_params=_cparams(("arbitrary",)),
        name="out_proj",
    )(o, w, y, mod)


def _ffn_kernel(y_ref, g_ref, shift_ref, scale_ref, gate_ref, wg_ref, wu_ref, wd_ref, out_ref,
                h_ref, acc_ref):
    j = pl.program_id(1)

    @pl.when(j == 0)
    def _():
        h_ref[...] = _modulate(y_ref[...], g_ref[...], shift_ref[...], scale_ref[...]).astype(BF16)
        acc_ref[...] = jnp.zeros_like(acc_ref)

    h = h_ref[...]
    gate = jnp.dot(h, wg_ref[...], preferred_element_type=F32)
    up = jnp.dot(h, wu_ref[...], preferred_element_type=F32)
    act = (gate * jax.nn.sigmoid(gate) * up).astype(BF16)
    acc_ref[...] += jnp.dot(act, wd_ref[...], preferred_element_type=F32)

    @pl.when(j == pl.num_programs(1) - 1)
    def _():
        out_ref[...] = y_ref[...] + gate_ref[...] * acc_ref[...]


def _ffn(y, g, mod, w_gu, w_down, layer, *, tm, tf, rows_per_batch):
    m_rows = y.shape[0]
    nf = D_FF // tf
    shift_spec, scale_spec, gate_spec = _mod_specs(mod, (3, 4, 5), tm, rows_per_batch)
    return pl.pallas_call(
        _ffn_kernel,
        grid=(m_rows // tm, nf),
        in_specs=[
            pl.BlockSpec((tm, D_MODEL), lambda i, j: (i, 0)),
            pl.BlockSpec((1, D_MODEL), lambda i, j: (0, 0)),
            shift_spec, scale_spec, gate_spec,
            pl.BlockSpec((None, D_MODEL, tf), lambda i, j: (layer, 0, j)),
            pl.BlockSpec((None, D_MODEL, tf), lambda i, j: (layer, 0, j + nf)),
            pl.BlockSpec((None, tf, D_MODEL), lambda i, j: (layer, j, 0)),
        ],
        out_specs=pl.BlockSpec((tm, D_MODEL), lambda i, j: (i, 0)),
        out_shape=jax.ShapeDtypeStruct((m_rows, D_MODEL), F32),
        scratch_shapes=[pltpu.VMEM((tm, D_MODEL), BF16), pltpu.VMEM((tm, D_MODEL), F32)],
        compiler_params=_cparams(("arbitrary", "arbitrary")),
        name="ffn",
    )(y, g, mod, mod, mod, w_gu, w_gu, w_down)


def _softmax_step(s, vt, m_ref, acc_ref, mi):
    m_old = m_ref[mi]
    m_new = jnp.maximum(m_old, jnp.max(s, axis=0, keepdims=True))
    p = jnp.exp2(s - m_new)
    acc_ref[mi] = jnp.exp2(m_old - m_new) * acc_ref[mi] + jnp.dot(vt, p.astype(BF16), preferred_element_type=F32)
    m_ref[mi] = m_new


def _diff_lambda(lam_ref, lam_init):
    lp = lam_ref[...]
    a = jnp.sum(lp[0:1] * lp[1:2], axis=-1, keepdims=True)
    b = jnp.sum(lp[2:3] * lp[3:4], axis=-1, keepdims=True)
    return jnp.exp(a) - jnp.exp(b) + lam_init


def _head_rms(o, subg, lam_init, axis):
    ms = jnp.mean(o * o, axis=axis, keepdims=True)
    return o * lax.rsqrt(ms + RMS_EPS) * subg * (1.0 - lam_init)


def _diff_flash_kernel(tab_ref, lam_ref, subg_ref, q_ref, k_ref, vt_ref, bias_ref, o_ref,
                       qz_ref, m_ref, acc_ref, *, t, lam_init):
    kvh = pl.program_id(1)
    qi = pl.program_id(2)
    nt = (((1,), (1,)), ((), ()))

    q = q_ref[...]
    lane = lax.broadcasted_iota(jnp.int32, (t, 2 * A_HD), 1)
    for g in range(2):
        qg = q[:, g * 2 * A_HD:(g + 1) * 2 * A_HD]
        qz_ref[0, g * t:(g + 1) * t, :] = jnp.where(lane < A_HD, qg, jnp.zeros_like(qg))
        qz_ref[1, g * t:(g + 1) * t, :] = jnp.where(lane >= A_HD, qg, jnp.zeros_like(qg))
    m_ref[...] = jnp.full(m_ref.shape, NEG, F32)
    acc_ref[...] = jnp.zeros(acc_ref.shape, F32)

    def scores(ki, kind):
        k = k_ref[pl.ds(pl.multiple_of(ki * t, t), t), :]
        ss = [lax.dot_general(k, qz_ref[mi], nt, preferred_element_type=F32) for mi in range(2)]
        return ss if kind is None else [bias_ref[kind, mi] + ss[mi] for mi in range(2)]

    def update(ki, ss):
        vt = vt_ref[:, pl.ds(pl.multiple_of(ki * t, t), t)]
        for mi in range(2):
            _softmax_step(ss[mi], vt, m_ref, acc_ref, mi)

    n_far = jnp.maximum(qi - 1, 0)

    def group(k0, kinds):
        ss = [scores(k0 + j, kind) for j, kind in enumerate(kinds)]
        for j in range(len(kinds)):
            update(k0 + j, ss[j])

    def far_quad(i, carry):
        group(4 * i, (None,) * 4)
        return carry

    lax.fori_loop(0, n_far // 4, far_quad, 0)
    rem = n_far % 4
    base = n_far - rem

    @pl.when(rem >= 2)
    def _():
        group(base, (None, None))

    @pl.when(rem % 2 == 1)
    def _():
        group(n_far - 1, (None,))

    col = lax.broadcasted_iota(jnp.int32, (1, 2 * t), 1)
    for mi in range(2):
        c0 = tab_ref[FAR_BUCKET, kvh * 4 + mi] * LOG2E
        c1 = tab_ref[FAR_BUCKET, kvh * 4 + 2 + mi] * LOG2E
        m_ref[mi] = m_ref[mi] + jnp.where(col < t, c0, c1)

    @pl.when(qi > 0)
    def _():
        group(qi - 1, (0, 1))

    @pl.when(qi == 0)
    def _():
        group(qi, (1,))

    o0 = acc_ref[0, 0:A_VD, :] * (1.0 / acc_ref[0, A_VD:A_VD + 1, :])
    o1 = acc_ref[1, 0:A_VD, :] * (1.0 / acc_ref[1, A_VD:A_VD + 1, :])
    o = _head_rms(o0 - _diff_lambda(lam_ref, lam_init) * o1, subg_ref[...], lam_init, axis=0).T
    for g in range(2):
        o_ref[:, g * A_VD:(g + 1) * A_VD] = o[g * t:(g + 1) * t, :].astype(BF16)


def _diff_flash(qkv, vt, bias_near, rel_bias, lam_p, sub_g, *, batch, seq, t, lam_init):
    assert t + 1 >= T5_SAT_DIST
    nq = seq // t
    kcol0 = A_Q // (2 * A_HD)
    return pl.pallas_call(
        functools.partial(_diff_flash_kernel, t=t, lam_init=lam_init),
        grid=(batch, A_KVH, nq),
        in_specs=[
            pl.BlockSpec(memory_space=pltpu.SMEM),
            pl.BlockSpec((4, A_HD), lambda b, h, i: (0, 0)),
            pl.BlockSpec((A_VD, 1), lambda b, h, i: (0, 0)),
            pl.BlockSpec((t, 2 * A_VD), lambda b, h, i, nq=nq: (b * nq + i, h)),
            pl.BlockSpec((seq, 2 * A_HD), lambda b, h, i: (b, kcol0 + h)),
            pl.BlockSpec((None, VT_ROWS, seq), lambda b, h, i: (h, 0, b)),
            pl.BlockSpec((2, None, 2, t, 2 * t), lambda b, h, i: (0, h, 0, 0, 0)),
        ],
        out_specs=pl.BlockSpec((t, 2 * A_VD), lambda b, h, i, nq=nq: (b * nq + i, h)),
        out_shape=jax.ShapeDtypeStruct((batch * seq, A_HEADS * A_VD), BF16),
        scratch_shapes=[
            pltpu.VMEM((2, 2 * t, 2 * A_HD), BF16),
            pltpu.VMEM((2, 1, 2 * t), F32),
            pltpu.VMEM((2, VT_ROWS, 2 * t), F32),
        ],
        compiler_params=_cparams(("arbitrary",) * 3),
        name="diff_flash",
    )(rel_bias, lam_p, sub_g.reshape(A_VD, 1), qkv, qkv, vt, bias_near)


def _swa_kernel(sink_ref, q_ref, kp_ref, kc_ref, vp_ref, vc_ref, bias_ref, o_ref):
    n = pl.program_id(1)
    w = WINDOW
    nt = (((1,), (1,)), ((), ()))
    q = q_ref[...]
    lane = lax.broadcasted_iota(jnp.int32, (w, 2 * B_HD), 1)
    key = lax.broadcasted_iota(jnp.int32, (1, 2 * w), 1)
    key_ok = key >= jnp.where(n > 0, 0, w)
    grp = lax.broadcasted_iota(jnp.int32, (B_GROUP * w, 1), 0) // w
    k_band = jnp.concatenate([kp_ref[...], kc_ref[...]], axis=0)
    v_band = jnp.concatenate([vp_ref[...], vc_ref[...]], axis=0)
    for kvh in range(B_KVH):
        cols = slice(kvh * B_HD, (kvh + 1) * B_HD)
        kk = k_band[:, cols]
        vv = v_band[:, cols]
        kk = jnp.concatenate([kk, kk], axis=1)
        vv = jnp.concatenate([vv, vv], axis=1)
        parts = []
        for j in range(B_GROUP // 2):
            chunk = q[:, (kvh * B_GROUP // 2 + j) * 2 * B_HD:(kvh * B_GROUP // 2 + j + 1) * 2 * B_HD]
            parts.append(jnp.where(lane < B_HD, chunk, jnp.zeros_like(chunk)))
            parts.append(jnp.where(lane >= B_HD, chunk, jnp.zeros_like(chunk)))
        qs = jnp.concatenate(parts, axis=0)
        s = lax.dot_general(qs, kk, nt, preferred_element_type=F32) + bias_ref[kvh]
        s = jnp.where(key_ok, s, NEG)
        sk = jnp.zeros((B_GROUP * w, 1), F32)
        for g in range(B_GROUP):
            sk = jnp.where(grp == g, sink_ref[kvh * B_GROUP + g], sk)
        mx = jnp.maximum(jnp.max(s, axis=-1, keepdims=True), sk)
        e = jnp.exp(s - mx)
        den = jnp.sum(e, axis=-1, keepdims=True) + jnp.exp(sk - mx)
        p = (e / den).astype(BF16)
        o = jnp.dot(p, vv, preferred_element_type=F32)
        for j in range(B_GROUP // 2):
            oa = o[(2 * j) * w:(2 * j + 1) * w, :]
            ob = o[(2 * j + 1) * w:(2 * j + 2) * w, :]
            c0 = (kvh * B_GROUP // 2 + j) * 2 * B_HD
            o_ref[:, c0:c0 + 2 * B_HD] = jnp.where(lane < B_HD, oa, ob).astype(BF16)


def _swa_prompt(qkv, bias_swa, sinks, *, batch, seq):
    nb = seq // WINDOW
    kblk = B_Q // B_K
    cur = lambda b, n: b * nb + n
    prev = lambda b, n: b * nb + jnp.maximum(n - 1, 0)
    return pl.pallas_call(
        _swa_kernel,
        grid=(batch, nb),
        in_specs=[
            pl.BlockSpec(memory_space=pltpu.SMEM),
            pl.BlockSpec((WINDOW, B_Q), lambda b, n: (cur(b, n), 0)),
            pl.BlockSpec((WINDOW, B_K), lambda b, n: (prev(b, n), kblk)),
            pl.BlockSpec((WINDOW, B_K), lambda b, n: (cur(b, n), kblk)),
            pl.BlockSpec((WINDOW, B_K), lambda b, n: (prev(b, n), kblk + 1)),
            pl.BlockSpec((WINDOW, B_K), lambda b, n: (cur(b, n), kblk + 1)),
            pl.BlockSpec(bias_swa.shape, lambda b, n: (0, 0, 0)),
        ],
        out_specs=pl.BlockSpec((WINDOW, B_Q), lambda b, n: (cur(b, n), 0)),
        out_shape=jax.ShapeDtypeStruct((batch * seq, B_Q), BF16),
        compiler_params=_cparams(("arbitrary", "arbitrary")),
        name="swa_prompt",
    )(sinks, qkv, qkv, qkv, qkv, qkv, bias_swa)


PAGES_PER_STEP = 8


def _paged_kernel(pt_ref, q_ref, knew_ref, vnew_ref, bias_far_ref, bias_last_ref, bias_self_ref, lam_ref, subg_ref,
                  *rest, lam_init):
    del pt_ref
    npg = PAGES_PER_STEP
    k_refs, v_refs = rest[:npg], rest[npg:2 * npg]
    o_ref, qb_ref, m_ref, l_ref, acc_ref = rest[2 * npg:]
    jg = pl.program_id(1)
    last = pl.num_programs(1) - 1
    nt = (((1,), (1,)), ((), ()))
    rows = 2 * A_HEADS
    pr = PAGE * A_KVH

    @pl.when(jg == 0)
    def _():
        q = q_ref[...]
        lane = lax.broadcasted_iota(jnp.int32, q.shape, 1)
        qb_ref[...] = jnp.concatenate([jnp.where(lane < A_HD, q, jnp.zeros_like(q)),
                                       jnp.where(lane >= A_HD, q, jnp.zeros_like(q))], axis=0)
        m_ref[...] = jnp.full(m_ref.shape, NEG, F32)
        l_ref[...] = jnp.zeros(l_ref.shape, F32)
        acc_ref[...] = jnp.zeros(acc_ref.shape, F32)

    qb = qb_ref[...]
    bias_far = bias_far_ref[...]
    bias_end = jnp.where(jg == last, bias_last_ref[...], bias_far)
    s = jnp.concatenate(
        [lax.dot_general(qb, k_refs[p][...].astype(BF16), nt, preferred_element_type=F32)
         + (bias_end if p == npg - 1 else bias_far) for p in range(npg)], axis=1)
    m_old = m_ref[...]
    m_new = jnp.maximum(m_old, jnp.max(s, axis=-1, keepdims=True))
    pf = jnp.exp(s - m_new)
    alpha = jnp.exp(m_old - m_new)
    l_ref[...] = alpha * l_ref[...] + jnp.sum(pf, axis=-1, keepdims=True)
    p = pf.astype(BF16)
    pv = jnp.dot(p[:, 0:pr], v_refs[0][...].astype(BF16), preferred_element_type=F32)
    for i in range(1, npg):
        pv = pv + jnp.dot(p[:, i * pr:(i + 1) * pr], v_refs[i][...].astype(BF16), preferred_element_type=F32)
    acc_ref[...] = alpha * acc_ref[...] + pv
    m_ref[...] = m_new

    @pl.when(jg == last)
    def _():
        s_self = jnp.sum(qb.astype(F32) * knew_ref[...], axis=-1, keepdims=True) + bias_self_ref[...]
        m_o = m_ref[...]
        m_n = jnp.maximum(m_o, s_self)
        p_self = jnp.exp(s_self - m_n)
        a = jnp.exp(m_o - m_n)
        l = a * l_ref[...] + p_self
        o = (a * acc_ref[...] + p_self * vnew_ref[...]) / l
        od = o[0:A_HEADS, :] - _diff_lambda(lam_ref, lam_init) * o[A_HEADS:rows, :]
        o_ref[...] = _head_rms(od, subg_ref[...], lam_init, axis=-1)


def _paged_decode(q, knew_rows, vnew_rows, cache_k, cache_v, layer, page_table, bias_far, bias_last, bias_self,
                  lam_p, sub_g, *, lam_init):
    nb, n_pages = page_table.shape
    npg = PAGES_PER_STEP
    rows = 2 * A_HEADS
    assert PAGE * (n_pages - 1) + 1 >= T5_SAT_DIST and PAGE == WINDOW
    pr = PAGE * A_KVH
    ck = cache_k.reshape(cache_k.shape[0], cache_k.shape[1], pr, 2 * A_HD)
    cv = cache_v.reshape(cache_v.shape[0], cache_v.shape[1], pr, A_VD)

    def page_spec(p):
        return pl.BlockSpec((None, None, pr, A_VD),
                            lambda b, j, pt, p=p: (layer, pt[b, j * npg + p], 0, 0))

    const2 = lambda b, j, pt: (0, 0)
    per_b = lambda b, j, pt: (b, 0, 0)
    grid_spec = pltpu.PrefetchScalarGridSpec(
        num_scalar_prefetch=1,
        grid=(nb, n_pages // npg),
        in_specs=[
            pl.BlockSpec((None, A_HEADS, 2 * A_HD), per_b),
            pl.BlockSpec((None, rows, 2 * A_HD), per_b),
            pl.BlockSpec((None, rows, A_VD), per_b),
            pl.BlockSpec((rows, pr), const2),
            pl.BlockSpec((rows, pr), const2),
            pl.BlockSpec((rows, 1), const2),
            pl.BlockSpec((4, A_HD), const2),
            pl.BlockSpec((1, A_VD), const2),
        ] + [page_spec(p) for p in range(npg)] * 2,
        out_specs=pl.BlockSpec((None, A_HEADS, A_VD), per_b),
        scratch_shapes=[
            pltpu.VMEM((rows, 2 * A_HD), BF16),
            pltpu.VMEM((rows, 1), F32),
            pltpu.VMEM((rows, 1), F32),
            pltpu.VMEM((rows, A_VD), F32),
        ],
    )
    return pl.pallas_call(
        functools.partial(_paged_kernel, lam_init=lam_init),
        grid_spec=grid_spec,
        out_shape=jax.ShapeDtypeStruct((nb, A_HEADS, A_VD), F32),
        compiler_params=_cparams(("arbitrary", "arbitrary")),
        name="paged_decode",
    )(page_table, q, knew_rows, vnew_rows, bias_far, bias_last, bias_self, lam_p, sub_g.reshape(1, A_VD),
      *([ck] * npg), *([cv] * npg))


def _swa_dec_kernel(q_ref, kbuf_ref, vbuf_ref, knew_ref, vnew_ref, bias_ref, sink_ref, o_ref):
    nt = (((1,), (1,)), ((), ()))
    q = q_ref[...]
    kbuf = kbuf_ref[...].astype(BF16)
    vbuf = vbuf_ref[...].astype(BF16)
    knew = knew_ref[...]
    vnew = vnew_ref[...]
    bias = bias_ref[...]
    sinks = sink_ref[...]
    for kvh in range(B_KVH):
        rows = slice(kvh * B_GROUP, (kvh + 1) * B_GROUP)
        cols = slice(kvh * B_HD, (kvh + 1) * B_HD)
        qh = q[rows, :]
        s = lax.dot_general(qh, kbuf[:, cols], nt, preferred_element_type=F32) + bias[rows, 0:WINDOW]
        s_new = (jnp.sum(qh.astype(F32) * knew[:, cols], axis=-1, keepdims=True)
                 + bias[rows, WINDOW:WINDOW + 1])
        sk = sinks[rows, :]
        mx = jnp.maximum(jnp.maximum(jnp.max(s, axis=-1, keepdims=True), s_new), sk)
        e = jnp.exp(s - mx)
        e_new = jnp.exp(s_new - mx)
        den = jnp.sum(e, axis=-1, keepdims=True) + e_new + jnp.exp(sk - mx)
        o = jnp.dot(e.astype(BF16), vbuf[:, cols], preferred_element_type=F32)
        o_ref[rows, :] = (o + e_new * vnew[:, cols]) / den


def _swa_decode(q, kbuf, vbuf, knew, vnew, bias_dec, sinks):
    nb = q.shape[0]
    per_b = lambda b: (b, 0, 0)
    return pl.pallas_call(
        _swa_dec_kernel,
        grid=(nb,),
        in_specs=[
            pl.BlockSpec((None, B_HEADS, B_HD), per_b),
            pl.BlockSpec((None, WINDOW, B_K), per_b),
            pl.BlockSpec((None, WINDOW, B_K), per_b),
            pl.BlockSpec((None, 1, B_K), per_b),
            pl.BlockSpec((None, 1, B_K), per_b),
            pl.BlockSpec(bias_dec.shape, lambda b: (0, 0)),
            pl.BlockSpec((B_HEADS, 1), lambda b: (0, 0)),
        ],
        out_specs=pl.BlockSpec((None, B_HEADS, B_HD), per_b),
        out_shape=jax.ShapeDtypeStruct((nb, B_HEADS, B_HD), F32),
        compiler_params=_cparams(("arbitrary",)),
        name="swa_decode",
    )(q, kbuf, vbuf, knew, vnew, bias_dec, sinks.reshape(B_HEADS, 1))


def _qk_gain_rows(qk_g, q_cols, kw, scale):
    gq = jnp.tile(qk_g[0].reshape(-1), q_cols // qk_g[0].size) * scale
    gk = jnp.tile(qk_g[1].reshape(-1), kw // qk_g[1].size)
    return jnp.concatenate([gq, gk, jnp.ones((kw,), F32)]).reshape(1, -1)


def kernel(x_prompt, x_sample, cache_k_a, cache_v_a, cache_win_k, cache_win_v, page_table, c_prompt, c_sample,
           rel_bias, w_ada, b_ada, g_attn, g_ffn, w_qkv_a, qk_g_a, lam_a, sub_g_a, w_o_a, w_qkv_b, qk_g_b,
           sinks_b, w_o_b, w_gate_up, w_down):
    batch, seq, _ = x_prompt.shape
    nb = x_sample.shape[0]
    n_pages = page_table.shape[1]
    t_attn = 256
    tm_p, tm_ffn, tf = 512, 512, 512

    n_c = nb + batch
    pad = (-n_c) % 8
    c_all = jnp.concatenate([c_sample, c_prompt, jnp.zeros((pad, D_MODEL), F32)], axis=0)
    mod = _ada_all(c_all, w_ada, b_ada)

    bias_near = _bias_near(rel_bias, t_attn)
    bias_swa = _bias_swa(rel_bias)
    bias_dec = _bias_dec(rel_bias)
    bias_dec_rows = bias_dec.reshape(A_HEADS, 2, 2 * WINDOW).transpose(1, 0, 2).reshape(2 * A_HEADS, 2 * WINDOW)
    row_kvh = (jnp.arange(2 * A_HEADS) % A_HEADS) // (A_HEADS // A_KVH)
    own = row_kvh[:, None] == (jnp.arange(PAGE * A_KVH) % A_KVH)[None, :]
    pb_far = jnp.where(own, bias_dec_rows[:, 0:1], NEG)
    pb_last = jnp.where(own, jnp.repeat(bias_dec_rows[:, 0:WINDOW], A_KVH, axis=1), NEG)
    pb_self = bias_dec_rows[:, WINDOW:WINDOW + 1]

    yp = x_prompt.reshape(batch * seq, D_MODEL)
    ys = x_sample.reshape(nb, D_MODEL)
    outs = {k: [] for k in ("kpa", "vpa", "ksa", "vsa", "kpb", "vpb", "ksb", "vsb")}

    w_qkv_a16, w_o_a16 = w_qkv_a.astype(BF16), w_o_a.astype(BF16)
    w_qkv_b16, w_o_b16 = w_qkv_b.astype(BF16), w_o_b.astype(BF16)
    w_gu16, w_dn16 = w_gate_up.astype(BF16), w_down.astype(BF16)

    for i in range(DEPTH):
        mod_p = mod[i, :, nb:nb + batch].transpose(1, 0, 2).reshape(batch, 6, 1, D_MODEL)
        mod_s = mod[i, :, :nb]
        g_a = g_attn[i].reshape(1, D_MODEL)
        g_f = g_ffn[i].reshape(1, D_MODEL)
        li = i // 2
        if i % 2 == 0:
            lam_init = 0.8 - 0.6 * math.exp(-0.3 * i)
            w_qkv, w_o = w_qkv_a16, w_o_a16
            gain = _qk_gain_rows(qk_g_a[li], A_Q, A_K, A_HD ** -0.5)
            gain_p = _qk_gain_rows(qk_g_a[li], A_Q, A_K, A_HD ** -0.5 * LOG2E)
            qkv_p, k_p, v_p, vt_p = _qkv_proj(yp, g_a, mod_p, w_qkv, li, gain_p, tm=tm_p, tn=A_K, kw=A_K,
                                              rows_per_batch=seq, emit_vt=True)
            o_p = _diff_flash(qkv_p, vt_p, bias_near, rel_bias, lam_a[li], sub_g_a[li], batch=batch, seq=seq,
                              t=t_attn, lam_init=lam_init)
            qkv_s, k_s, v_s = _qkv_proj(ys, g_a, mod_s, w_qkv, li, gain, tm=nb, tn=A_K, kw=A_K,
                                        rows_per_batch=None)
            kv_rows = lambda a: jnp.tile(jnp.repeat(a.reshape(nb, A_KVH, A_VD), A_HEADS // A_KVH, axis=1), (1, 2, 1))
            o_s = _paged_decode(qkv_s[:, :A_Q].reshape(nb, A_HEADS, 2 * A_HD), kv_rows(k_s), kv_rows(v_s),
                                cache_k_a, cache_v_a, li, page_table, pb_far, pb_last, pb_self,
                                lam_a[li], sub_g_a[li], lam_init=lam_init)
            o_s = o_s.reshape(nb, A_HEADS * A_VD).astype(BF16)
            outs["kpa"].append(k_p.reshape(batch, seq // PAGE, PAGE, A_KVH, 2 * A_HD))
            outs["vpa"].append(v_p.reshape(batch, seq // PAGE, PAGE, A_KVH, A_VD))
            outs["ksa"].append(k_s.reshape(nb, 1, A_KVH, 2 * A_HD))
            outs["vsa"].append(v_s.reshape(nb, 1, A_KVH, A_VD))
        else:
            w_qkv, w_o = w_qkv_b16, w_o_b16
            gain = _qk_gain_rows(qk_g_b[li], B_Q, B_K, B_HD ** -0.5)
            qkv_p, k_p, v_p = _qkv_proj(yp, g_a, mod_p, w_qkv, li, gain, tm=tm_p, tn=2 * B_K, kw=B_K,
                                        rows_per_batch=seq)
            o_p = _swa_prompt(qkv_p, bias_swa, sinks_b[li], batch=batch, seq=seq)
            qkv_s, k_s, v_s = _qkv_proj(ys, g_a, mod_s, w_qkv, li, gain, tm=nb, tn=2 * B_K, kw=B_K,
                                        rows_per_batch=None)
            buf_k = cache_win_k[li].reshape(nb, WINDOW, B_K)
            buf_v = cache_win_v[li].reshape(nb, WINDOW, B_K)
            o_s = _swa_decode(qkv_s[:, :B_Q].reshape(nb, B_HEADS, B_HD), buf_k, buf_v, k_s.reshape(nb, 1, B_K),
                              v_s.reshape(nb, 1, B_K), bias_dec, sinks_b[li])
            o_s = o_s.reshape(nb, B_Q).astype(BF16)
            last_w = lambda a: a.reshape(batch, seq, B_K)[:, seq - WINDOW:].reshape(batch, WINDOW, B_KVH, B_HD)
            outs["kpb"].append(last_w(k_p))
            outs["vpb"].append(last_w(v_p))
            outs["ksb"].append(jnp.concatenate([buf_k[:, 1:], k_s.reshape(nb, 1, B_K)], axis=1)
                               .reshape(nb, WINDOW, B_KVH, B_HD))
            outs["vsb"].append(jnp.concatenate([buf_v[:, 1:], v_s.reshape(nb, 1, B_K)], axis=1)
                               .reshape(nb, WINDOW, B_KVH, B_HD))
        yp = _out_proj(o_p, w_o, li, yp, mod_p, tm=tm_p, rows_per_batch=seq)
        ys = _out_proj(o_s, w_o, li, ys, mod_s, tm=nb, rows_per_batch=None)
        yp = _ffn(yp, g_f, mod_p, w_gu16, w_dn16, i, tm=tm_ffn, tf=tf, rows_per_batch=seq)
        ys = _ffn(ys, g_f, mod_s, w_gu16, w_dn16, i, tm=nb, tf=tf, rows_per_batch=None)

    return (yp.reshape(batch, seq, D_MODEL), ys.reshape(nb, 1, D_MODEL),
            jnp.stack(outs["kpa"]), jnp.stack(outs["vpa"]), jnp.stack(outs["ksa"]), jnp.stack(outs["vsa"]),
            jnp.stack(outs["kpb"]), jnp.stack(outs["vpb"]), jnp.stack(outs["ksb"]), jnp.stack(outs["vsb"]))
```

```python
import functools
import math

import jax
import jax.numpy as jnp
import numpy as np
from jax import lax
from jax.experimental import pallas as pl
from jax.experimental.pallas import tpu as pltpu

F32 = jnp.float32
BF16 = jnp.bfloat16

D_MODEL = 2048
DEPTH = 4
PAGE = 128
A_HEADS, A_KVH, A_HD, A_VD = 16, 8, 64, 128
A_Q = A_HEADS * 2 * A_HD
A_K = A_KVH * 2 * A_HD
A_V = A_KVH * A_VD
B_HEADS, B_KVH, B_HD = 32, 4, 64
B_GROUP = B_HEADS // B_KVH
B_Q = B_HEADS * B_HD
B_K = B_KVH * B_HD
WINDOW = 128
D_FF = 5632
NUM_BUCKETS = 32
MAX_DISTANCE = 128
MAX_EXACT = NUM_BUCKETS // 2
RMS_EPS = 1e-6
LOG2E = math.log2(math.e)
VT_ROWS = A_VD + 16
NEG = -1e30
QK_GROUP = 64

VMEM_LIMIT = 56 * 1024 * 1024
PAGES_PER_SOFTMAX = 4


def _t5_saturation_distance():
    d = np.arange(0, 4 * MAX_DISTANCE, dtype=np.int32)
    nf = np.maximum(d, MAX_EXACT).astype(np.float32)
    large = MAX_EXACT + (np.log(nf / MAX_EXACT) / math.log(MAX_DISTANCE / MAX_EXACT)
                         * (NUM_BUCKETS - MAX_EXACT)).astype(np.int32)
    bucket = np.where(d < MAX_EXACT, d, np.minimum(large, NUM_BUCKETS - 1))
    below = np.nonzero(bucket < NUM_BUCKETS - 1)[0]
    return int(below.max()) + 1


T5_SAT_DIST = _t5_saturation_distance()
FAR_BUCKET = NUM_BUCKETS - 1


def _cparams(sem):
    return pltpu.CompilerParams(dimension_semantics=sem, vmem_limit_bytes=VMEM_LIMIT)


def _bucket(dist):
    n = jnp.maximum(dist, 0)
    nf = jnp.maximum(n, MAX_EXACT).astype(F32)
    large = MAX_EXACT + (jnp.log(nf / MAX_EXACT) / math.log(MAX_DISTANCE / MAX_EXACT)
                         * (NUM_BUCKETS - MAX_EXACT)).astype(jnp.int32)
    return jnp.where(n < MAX_EXACT, n, jnp.minimum(large, NUM_BUCKETS - 1))


def _table_lookup(bucket, tab_ref, col):
    acc = jnp.zeros(bucket.shape, F32)
    for b in range(NUM_BUCKETS):
        acc = jnp.where(bucket == b, tab_ref[b, col], acc)
    return acc


def _ada_kernel(c_ref, w_ref, b_ref, o_ref):
    c = c_ref[...]
    a = (c * jax.nn.sigmoid(c)).astype(BF16)
    o_ref[...] = jnp.dot(a, w_ref[...].astype(BF16), preferred_element_type=F32) + b_ref[...]


def _ada_all(c_all, w_ada, b_ada, tn=1024):
    mp = c_all.shape[0]
    per_chunk = D_MODEL // tn
    return pl.pallas_call(
        _ada_kernel,
        grid=(DEPTH, 6 * per_chunk),
        in_specs=[
            pl.BlockSpec((mp, D_MODEL), lambda l, j: (0, 0)),
            pl.BlockSpec((None, D_MODEL, tn), lambda l, j: (l, 0, j)),
            pl.BlockSpec((None, 1, tn), lambda l, j: (l, 0, j)),
        ],
        out_specs=pl.BlockSpec((None, None, mp, tn), lambda l, j: (l, j // per_chunk, 0, j % per_chunk)),
        out_shape=jax.ShapeDtypeStruct((DEPTH, 6, mp, D_MODEL), F32),
        compiler_params=_cparams(("arbitrary", "arbitrary")),
        name="ada_mod",
    )(c_all, w_ada, b_ada.reshape(DEPTH, 1, 6 * D_MODEL))


def _bias_near_kernel(tab_ref, o_ref, *, t):
    kind = pl.program_id(0)
    kvh = pl.program_id(1)
    m = pl.program_id(2)
    key = lax.broadcasted_iota(jnp.int32, (t, t), 0)
    qry = lax.broadcasted_iota(jnp.int32, (t, t), 1)
    dist = qry - key + (1 - kind) * t
    bucket = _bucket(dist)
    for g in range(2):
        col = kvh * 4 + g * 2 + m
        tile = _table_lookup(bucket, tab_ref, col)
        o_ref[:, g * t:(g + 1) * t] = jnp.where(dist >= 0, tile * LOG2E, NEG)


def _bias_near(rel_bias, t):
    return pl.pallas_call(
        functools.partial(_bias_near_kernel, t=t),
        grid=(2, A_KVH, 2),
        in_specs=[pl.BlockSpec(memory_space=pltpu.SMEM)],
        out_specs=pl.BlockSpec((None, None, None, t, 2 * t), lambda k, h, m: (k, h, m, 0, 0)),
        out_shape=jax.ShapeDtypeStruct((2, A_KVH, 2, t, 2 * t), F32),
        compiler_params=_cparams(("arbitrary",) * 3),
        name="bias_near",
    )(rel_bias)


def _bias_swa_kernel(tab_ref, o_ref):
    kvh = pl.program_id(0)
    q = lax.broadcasted_iota(jnp.int32, (WINDOW, 2 * WINDOW), 0)
    s = lax.broadcasted_iota(jnp.int32, (WINDOW, 2 * WINDOW), 1)
    dist = q + WINDOW - s
    bucket = _bucket(dist)
    valid = (dist >= 0) & (dist <= WINDOW)
    for g in range(B_GROUP):
        tile = _table_lookup(bucket, tab_ref, kvh * B_GROUP + g)
        o_ref[g * WINDOW:(g + 1) * WINDOW, :] = jnp.where(valid, tile, NEG)


def _bias_swa(rel_bias):
    return pl.pallas_call(
        _bias_swa_kernel,
        grid=(B_KVH,),
        in_specs=[pl.BlockSpec(memory_space=pltpu.SMEM)],
        out_specs=pl.BlockSpec((None, B_GROUP * WINDOW, 2 * WINDOW), lambda h: (h, 0, 0)),
        out_shape=jax.ShapeDtypeStruct((B_KVH, B_GROUP * WINDOW, 2 * WINDOW), F32),
        compiler_params=_cparams(("arbitrary",)),
        name="bias_swa",
    )(rel_bias)


def _bias_dec_kernel(tab_t_ref, o_ref):
    s = lax.broadcasted_iota(jnp.int32, (NUM_BUCKETS, 2 * WINDOW), 1)
    bucket = _bucket(WINDOW - s)
    tab_t = tab_t_ref[...]
    acc = jnp.zeros((NUM_BUCKETS, 2 * WINDOW), F32)
    for b in range(NUM_BUCKETS):
        acc = jnp.where(bucket == b, tab_t[:, b:b + 1], acc)
    o_ref[...] = acc


def _bias_dec(rel_bias):
    return pl.pallas_call(
        _bias_dec_kernel,
        out_shape=jax.ShapeDtypeStruct((NUM_BUCKETS, 2 * WINDOW), F32),
        name="bias_dec",
    )(rel_bias.T)


def _modulate(x, g, shift, scale):
    ms = jnp.mean(x * x, axis=-1, keepdims=True)
    h = x * lax.rsqrt(ms + RMS_EPS) * g
    return h * (1.0 + scale) + shift


def _mod_specs(mod, chunks, tm, rows_per_batch):
    specs = []
    for c in chunks:
        if mod.ndim == 4:
            tiles = rows_per_batch // tm
            specs.append(pl.BlockSpec((None, None, 1, D_MODEL),
                                      lambda i, *_, c=c, tiles=tiles: (i // tiles, c, 0, 0)))
        else:
            specs.append(pl.BlockSpec((None, tm, D_MODEL), lambda i, *_, c=c: (c, i, 0)))
    return specs


def _qkv_kernel(x_ref, g_ref, shift_ref, scale_ref, w_ref, gain_ref, p_ref, ob_ref, k_ref, v_ref, *rest,
                tn, kw, n_q_tiles, n_tiles, emit_vt):
    vt_ref = rest[0] if emit_vt else None
    h_ref = rest[-1]
    j = pl.program_id(1)
    pw = p_ref.shape[0]

    @pl.when(j == 0)
    def _():
        h_ref[...] = _modulate(x_ref[...], g_ref[...], shift_ref[...], scale_ref[...]).astype(BF16)

    def tile(parts):
        h = h_ref[...]
        proj = lambda c: jnp.dot(h, w_ref[:, c * pw:(c + 1) * pw], preferred_element_type=F32)
        nxt = proj(0)
        for c, (kind, off) in enumerate(parts):
            cols = slice(c * pw, (c + 1) * pw)
            out = nxt
            if c + 1 < len(parts):
                nxt = proj(c + 1)
            if kind != "v":
                ms = jnp.dot((out * out).astype(BF16), p_ref[...], preferred_element_type=F32)
                out = out * lax.rsqrt(ms + RMS_EPS) * gain_ref[:, cols]
            ob_ref[:, cols] = out.astype(BF16)
            if kind == "k":
                k_ref[:, off:off + pw] = out
            if kind == "v":
                v_ref[:, off:off + pw] = out
                if emit_vt:
                    vt = out.T.astype(BF16)
                    for i in range(pw // A_VD):
                        head = off // A_VD + i
                        vt_ref[head, 0:A_VD, :] = vt[i * A_VD:(i + 1) * A_VD, :]
                        vt_ref[head, A_VD:VT_ROWS, :] = jnp.ones((VT_ROWS - A_VD, vt.shape[1]), BF16)

    @pl.when(j < n_q_tiles)
    def _():
        tile([("q", 0)] * (tn // pw))

    for jj in range(n_q_tiles, n_tiles):
        offs = [(jj - n_q_tiles) * tn + c * pw for c in range(tn // pw)]
        parts = [("k", o) if o < kw else ("v", o - kw) for o in offs]
        pl.when(j == jj)(functools.partial(tile, parts))


def _qkv_proj(x, g, mod, w, layer, gain, *, tm, tn, kw, rows_per_batch, emit_vt=False):
    m_rows, n = x.shape[0], w.shape[2]
    q_cols = n - 2 * kw
    pw = 256
    assert q_cols % tn == 0 and tn % pw == 0 and kw % pw == 0
    idx = np.arange(pw)
    pmat = jnp.asarray((idx[:, None] // QK_GROUP == idx[None, :] // QK_GROUP) / QK_GROUP, BF16)
    shift_spec, scale_spec = _mod_specs(mod, (0, 1), tm, rows_per_batch)
    return pl.pallas_call(
        functools.partial(_qkv_kernel, tn=tn, kw=kw, n_q_tiles=q_cols // tn, n_tiles=n // tn, emit_vt=emit_vt),
        grid=(m_rows // tm, n // tn),
        in_specs=[
            pl.BlockSpec((tm, D_MODEL), lambda i, j: (i, 0)),
            pl.BlockSpec((1, D_MODEL), lambda i, j: (0, 0)),
            shift_spec, scale_spec,
            pl.BlockSpec((None, D_MODEL, tn), lambda i, j: (layer, 0, j)),
            pl.BlockSpec((1, tn), lambda i, j: (0, j)),
            pl.BlockSpec((pw, pw), lambda i, j: (0, 0)),
        ],
        out_specs=[
            pl.BlockSpec((tm, tn), lambda i, j: (i, j)),
            pl.BlockSpec((tm, kw), lambda i, j: (i, 0)),
            pl.BlockSpec((tm, kw), lambda i, j: (i, 0)),
        ] + ([pl.BlockSpec((A_KVH, VT_ROWS, tm), lambda i, j: (0, 0, i))] if emit_vt else []),
        out_shape=[
            jax.ShapeDtypeStruct((m_rows, n), BF16),
            jax.ShapeDtypeStruct((m_rows, kw), F32),
            jax.ShapeDtypeStruct((m_rows, kw), F32),
        ] + ([jax.ShapeDtypeStruct((A_KVH, VT_ROWS, m_rows), BF16)] if emit_vt else []),
        scratch_shapes=[pltpu.VMEM((tm, D_MODEL), BF16)],
        compiler_params=_cparams(("arbitrary", "arbitrary")),
        name="qkv_proj",
    )(x, g, mod, mod, w, gain, pmat)


def _oproj_kernel(o_ref, w_ref, y_ref, gate_ref, out_ref):
    out_ref[...] = y_ref[...] + gate_ref[...] * jnp.dot(o_ref[...], w_ref[...], preferred_element_type=F32)


def _out_proj(o, w, layer, y, mod, *, tm, rows_per_batch):
    m_rows = y.shape[0]
    (gate_spec,) = _mod_specs(mod, (2,), tm, rows_per_batch)
    return pl.pallas_call(
        _oproj_kernel,
        grid=(m_rows // tm,),
        in_specs=[
            pl.BlockSpec((tm, o.shape[1]), lambda i: (i, 0)),
            pl.BlockSpec((None,) + w.shape[1:], lambda i: (layer, 0, 0)),
            pl.BlockSpec((tm, D_MODEL), lambda i: (i, 0)),
            gate_spec,
        ],
        out_specs=pl.BlockSpec((tm, D_MODEL), lambda i: (i, 0)),
        out_shape=jax.ShapeDtypeStruct((m_rows, D_MODEL), F32),
        compiler_params=_cparams(("arbitrary",)),
        name="out_proj",
    )(o, w, y, mod)


def _ffn_kernel(y_ref, g_ref, shift_ref, scale_ref, gate_ref, wg_ref, wu_ref, wd_ref, out_ref,
                h_ref, acc_ref):
    j = pl.program_id(1)

    @pl.when(j == 0)
    def _():
        h_ref[...] = _modulate(y_ref[...], g_ref[...], shift_ref[...], scale_ref[...]).astype(BF16)
        acc_ref[...] = jnp.zeros_like(acc_ref)

    h = h_ref[...]
    gate = jnp.dot(h, wg_ref[...], preferred_element_type=F32)
    up = jnp.dot(h, wu_ref[...], preferred_element_type=F32)
    act = (gate * jax.nn.sigmoid(gate) * up).astype(BF16)
    acc_ref[...] += jnp.dot(act, wd_ref[...], preferred_element_type=F32)

    @pl.when(j == pl.num_programs(1) - 1)
    def _():
        out_ref[...] = y_ref[...] + gate_ref[...] * acc_ref[...]


def _ffn(y, g, mod, w_gu, w_down, layer, *, tm, tf, rows_per_batch):
    m_rows = y.shape[0]
    nf = D_FF // tf
    shift_spec, scale_spec, gate_spec = _mod_specs(mod, (3, 4, 5), tm, rows_per_batch)
    return pl.pallas_call(
        _ffn_kernel,
        grid=(m_rows // tm, nf),
        in_specs=[
            pl.BlockSpec((tm, D_MODEL), lambda i, j: (i, 0)),
            pl.BlockSpec((1, D_MODEL), lambda i, j: (0, 0)),
            shift_spec, scale_spec, gate_spec,
            pl.BlockSpec((None, D_MODEL, tf), lambda i, j: (layer, 0, j)),
            pl.BlockSpec((None, D_MODEL, tf), lambda i, j: (layer, 0, j + nf)),
            pl.BlockSpec((None, tf, D_MODEL), lambda i, j: (layer, j, 0)),
        ],
        out_specs=pl.BlockSpec((tm, D_MODEL), lambda i, j: (i, 0)),
        out_shape=jax.ShapeDtypeStruct((m_rows, D_MODEL), F32),
        scratch_shapes=[pltpu.VMEM((tm, D_MODEL), BF16), pltpu.VMEM((tm, D_MODEL), F32)],
        compiler_params=_cparams(("arbitrary", "arbitrary")),
        name="ffn",
    )(y, g, mod, mod, mod, w_gu, w_gu, w_down)


def _softmax_step(s, vt, m_ref, acc_ref, mi):
    m_old = m_ref[mi]
    m_new = jnp.maximum(m_old, jnp.max(s, axis=0, keepdims=True))
    p = jnp.exp2(s - m_new)
    acc_ref[mi] = jnp.exp2(m_old - m_new) * acc_ref[mi] + jnp.dot(vt, p.astype(BF16), preferred_element_type=F32)
    m_ref[mi] = m_new


def _diff_lambda(lam_ref, lam_init):
    lp = lam_ref[...]
    a = jnp.sum(lp[0:1] * lp[1:2], axis=-1, keepdims=True)
    b = jnp.sum(lp[2:3] * lp[3:4], axis=-1, keepdims=True)
    return jnp.exp(a) - jnp.exp(b) + lam_init


def _head_rms(o, subg, lam_init, axis):
    ms = jnp.mean(o * o, axis=axis, keepdims=True)
    return o * lax.rsqrt(ms + RMS_EPS) * subg * (1.0 - lam_init)


def _diff_flash_body(kvh, qi, tab_ref, lam_ref, subg_ref, q_ref, k_ref, vt_ref, bias_ref, o_ref,
                     qz_ref, m_ref, acc_ref, *, t, lam_init):
    nt = (((1,), (1,)), ((), ()))

    q = q_ref[...]
    lane = lax.broadcasted_iota(jnp.int32, (t, 2 * A_HD), 1)
    for g in range(2):
        qg = q[:, g * 2 * A_HD:(g + 1) * 2 * A_HD]
        qz_ref[0, g * t:(g + 1) * t, :] = jnp.where(lane < A_HD, qg, jnp.zeros_like(qg))
        qz_ref[1, g * t:(g + 1) * t, :] = jnp.where(lane >= A_HD, qg, jnp.zeros_like(qg))
    m_ref[...] = jnp.full(m_ref.shape, NEG, F32)
    acc_ref[...] = jnp.zeros(acc_ref.shape, F32)

    def scores(ki, kind):
        k = k_ref[pl.ds(pl.multiple_of(ki * t, t), t), :]
        ss = [lax.dot_general(k, qz_ref[mi], nt, preferred_element_type=F32) for mi in range(2)]
        return ss if kind is None else [bias_ref[kind, mi] + ss[mi] for mi in range(2)]

    def update(ki, ss):
        vt = vt_ref[:, pl.ds(pl.multiple_of(ki * t, t), t)]
        for mi in range(2):
            _softmax_step(ss[mi], vt, m_ref, acc_ref, mi)

    n_far = jnp.maximum(qi - 1, 0)

    def group(k0, kinds):
        ss = [scores(k0 + j, kind) for j, kind in enumerate(kinds)]
        for j in range(len(kinds)):
            update(k0 + j, ss[j])

    def far_quad(i, carry):
        group(4 * i, (None,) * 4)
        return carry

    lax.fori_loop(0, n_far // 4, far_quad, 0)
    rem = n_far % 4
    base = n_far - rem

    @pl.when(rem >= 2)
    def _():
        group(base, (None, None))

    @pl.when(rem % 2 == 1)
    def _():
        group(n_far - 1, (None,))

    col = lax.broadcasted_iota(jnp.int32, (1, 2 * t), 1)
    for mi in range(2):
        c0 = tab_ref[FAR_BUCKET, kvh * 4 + mi] * LOG2E
        c1 = tab_ref[FAR_BUCKET, kvh * 4 + 2 + mi] * LOG2E
        m_ref[mi] = m_ref[mi] + jnp.where(col < t, c0, c1)

    @pl.when(qi > 0)
    def _():
        group(qi - 1, (0, 1))

    @pl.when(qi == 0)
    def _():
        group(qi, (1,))

    o0 = acc_ref[0, 0:A_VD, :] * (1.0 / acc_ref[0, A_VD:A_VD + 1, :])
    o1 = acc_ref[1, 0:A_VD, :] * (1.0 / acc_ref[1, A_VD:A_VD + 1, :])
    o = _head_rms(o0 - _diff_lambda(lam_ref, lam_init) * o1, subg_ref[...], lam_init, axis=0).T
    for g in range(2):
        o_ref[:, g * A_VD:(g + 1) * A_VD] = o[g * t:(g + 1) * t, :].astype(BF16)


def _swa_kernel(sink_ref, q_ref, kp_ref, kc_ref, vp_ref, vc_ref, bias_ref, o_ref):
    n = pl.program_id(1)
    w = WINDOW
    nt = (((1,), (1,)), ((), ()))
    q = q_ref[...]
    lane = lax.broadcasted_iota(jnp.int32, (w, 2 * B_HD), 1)
    key = lax.broadcasted_iota(jnp.int32, (1, 2 * w), 1)
    key_ok = key >= jnp.where(n > 0, 0, w)
    grp = lax.broadcasted_iota(jnp.int32, (B_GROUP * w, 1), 0) // w
    k_band = jnp.concatenate([kp_ref[...], kc_ref[...]], axis=0)
    v_band = jnp.concatenate([vp_ref[...], vc_ref[...]], axis=0)
    for kvh in range(B_KVH):
        cols = slice(kvh * B_HD, (kvh + 1) * B_HD)
        kk = k_band[:, cols]
        vv = v_band[:, cols]
        kk = jnp.concatenate([kk, kk], axis=1)
        vv = jnp.concatenate([vv, vv], axis=1)
        parts = []
        for j in range(B_GROUP // 2):
            chunk = q[:, (kvh * B_GROUP // 2 + j) * 2 * B_HD:(kvh * B_GROUP // 2 + j + 1) * 2 * B_HD]
            parts.append(jnp.where(lane < B_HD, chunk, jnp.zeros_like(chunk)))
            parts.append(jnp.where(lane >= B_HD, chunk, jnp.zeros_like(chunk)))
        qs = jnp.concatenate(parts, axis=0)
        s = lax.dot_general(qs, kk, nt, preferred_element_type=F32) + bias_ref[kvh]
        s = jnp.where(key_ok, s, NEG)
        sk = jnp.zeros((B_GROUP * w, 1), F32)
        for g in range(B_GROUP):
            sk = jnp.where(grp == g, sink_ref[kvh * B_GROUP + g], sk)
        mx = jnp.maximum(jnp.max(s, axis=-1, keepdims=True), sk)
        e = jnp.exp(s - mx)
        den = jnp.sum(e, axis=-1, keepdims=True) + jnp.exp(sk - mx)
        p = (e / den).astype(BF16)
        o = jnp.dot(p, vv, preferred_element_type=F32)
        for j in range(B_GROUP // 2):
            oa = o[(2 * j) * w:(2 * j + 1) * w, :]
            ob = o[(2 * j + 1) * w:(2 * j + 2) * w, :]
            c0 = (kvh * B_GROUP // 2 + j) * 2 * B_HD
            o_ref[:, c0:c0 + 2 * B_HD] = jnp.where(lane < B_HD, oa, ob).astype(BF16)


def _swa_prompt(qkv, bias_swa, sinks, *, batch, seq):
    nb = seq // WINDOW
    kblk = B_Q // B_K
    cur = lambda b, n: b * nb + n
    prev = lambda b, n: b * nb + jnp.maximum(n - 1, 0)
    return pl.pallas_call(
        _swa_kernel,
        grid=(batch, nb),
        in_specs=[
            pl.BlockSpec(memory_space=pltpu.SMEM),
            pl.BlockSpec((WINDOW, B_Q), lambda b, n: (cur(b, n), 0)),
            pl.BlockSpec((WINDOW, B_K), lambda b, n: (prev(b, n), kblk)),
            pl.BlockSpec((WINDOW, B_K), lambda b, n: (cur(b, n), kblk)),
            pl.BlockSpec((WINDOW, B_K), lambda b, n: (prev(b, n), kblk + 1)),
            pl.BlockSpec((WINDOW, B_K), lambda b, n: (cur(b, n), kblk + 1)),
            pl.BlockSpec(bias_swa.shape, lambda b, n: (0, 0, 0)),
        ],
        out_specs=pl.BlockSpec((WINDOW, B_Q), lambda b, n: (cur(b, n), 0)),
        out_shape=jax.ShapeDtypeStruct((batch * seq, B_Q), BF16),
        compiler_params=_cparams(("arbitrary", "arbitrary")),
        name="swa_prompt",
    )(sinks, qkv, qkv, qkv, qkv, qkv, bias_swa)


def _paged_body(jg, last, q_ref, knew_ref, vnew_ref, bias_far_ref, bias_last_ref, bias_self_ref, lam_ref, subg_ref,
                k_refs, v_refs, o_ref, qb_ref, m_ref, l_ref, acc_ref, *, lam_init):
    npg = len(k_refs)
    nt = (((1,), (1,)), ((), ()))
    rows = 2 * A_HEADS
    pr = PAGE * A_KVH

    @pl.when(jg == 0)
    def _():
        q = q_ref[...]
        lane = lax.broadcasted_iota(jnp.int32, q.shape, 1)
        qb_ref[...] = jnp.concatenate([jnp.where(lane < A_HD, q, jnp.zeros_like(q)),
                                       jnp.where(lane >= A_HD, q, jnp.zeros_like(q))], axis=0)
        m_ref[...] = jnp.full(m_ref.shape, NEG, F32)
        l_ref[...] = jnp.zeros(l_ref.shape, F32)
        acc_ref[...] = jnp.zeros(acc_ref.shape, F32)

    qb = qb_ref[...]
    bias_far = bias_far_ref[...]
    bias_end = jnp.where(jg == last, bias_last_ref[...], bias_far)
    sub = min(npg, PAGES_PER_SOFTMAX)

    def scores(g):
        return jnp.concatenate(
            [lax.dot_general(qb, k_refs[p][...].astype(BF16), nt, preferred_element_type=F32)
             + (bias_end if p == npg - 1 else bias_far) for p in range(g * sub, (g + 1) * sub)], axis=1)

    m, l, acc = m_ref[...], l_ref[...], acc_ref[...]
    nxt = scores(0)
    for g in range(npg // sub):
        s = nxt
        if g + 1 < npg // sub:
            nxt = scores(g + 1)
        m_new = jnp.maximum(m, jnp.max(s, axis=-1, keepdims=True))
        pf = jnp.exp(s - m_new)
        alpha = jnp.exp(m - m_new)
        l = alpha * l + jnp.sum(pf, axis=-1, keepdims=True)
        p = pf.astype(BF16)
        pv = jnp.dot(p[:, 0:pr], v_refs[g * sub][...].astype(BF16), preferred_element_type=F32)
        for i in range(1, sub):
            pv = pv + jnp.dot(p[:, i * pr:(i + 1) * pr], v_refs[g * sub + i][...].astype(BF16),
                              preferred_element_type=F32)
        acc = alpha * acc + pv
        m = m_new
    m_ref[...], l_ref[...], acc_ref[...] = m, l, acc

    @pl.when(jg == last)
    def _():
        s_self = jnp.sum(qb.astype(F32) * knew_ref[...], axis=-1, keepdims=True) + bias_self_ref[...]
        m_o = m_ref[...]
        m_n = jnp.maximum(m_o, s_self)
        p_self = jnp.exp(s_self - m_n)
        a = jnp.exp(m_o - m_n)
        l = a * l_ref[...] + p_self
        o = (a * acc_ref[...] + p_self * vnew_ref[...]) / l
        od = o[0:A_HEADS, :] - _diff_lambda(lam_ref, lam_init) * o[A_HEADS:rows, :]
        o_ref[...] = _head_rms(od, subg_ref[...], lam_init, axis=-1)


def _attn_a_kernel(pt_ref, tab_ref, lam_ref, subg_col_ref, subg_row_ref, q_ref, k_ref, vt_ref, bias_ref,
                   qs_ref, knew_ref, vnew_ref, pb_far_ref, pb_last_ref, pb_self_ref, *rest,
                   t, lam_init, npg, steps_per_token):
    del pt_ref
    k_pages, v_pages = rest[:npg], rest[npg:2 * npg]
    o_ref, os_ref, qz_ref, m_ref, acc_ref, qb_ref, dm_ref, dl_ref, dacc_ref = rest[2 * npg:]
    kvh = pl.program_id(1)
    qi = pl.program_id(2)
    flat = (pl.program_id(0) * pl.num_programs(1) + kvh) * pl.num_programs(2) + qi
    _paged_body(flat % steps_per_token, steps_per_token - 1, qs_ref, knew_ref, vnew_ref, pb_far_ref, pb_last_ref,
                pb_self_ref, lam_ref, subg_row_ref, k_pages, v_pages, os_ref, qb_ref, dm_ref, dl_ref, dacc_ref,
                lam_init=lam_init)
    _diff_flash_body(kvh, qi, tab_ref, lam_ref, subg_col_ref, q_ref, k_ref, vt_ref, bias_ref, o_ref,
                     qz_ref, m_ref, acc_ref, t=t, lam_init=lam_init)


def _attn_a(qkv, vt, bias_near, rel_bias, lam_p, sub_g, q_s, knew_rows, vnew_rows, cache_k, cache_v, layer,
            page_table, pb_far, pb_last, pb_self, *, batch, seq, t, lam_init):
    assert t + 1 >= T5_SAT_DIST
    nb, n_pages = page_table.shape
    assert PAGE * (n_pages - 1) + 1 >= T5_SAT_DIST and PAGE == WINDOW
    nq = seq // t
    n_steps = batch * A_KVH * nq
    npg = nb * n_pages // n_steps
    assert npg * n_steps == nb * n_pages and n_pages % npg == 0 and npg % min(npg, PAGES_PER_SOFTMAX) == 0
    spt = n_pages // npg
    kcol0 = A_Q // (2 * A_HD)
    rows = 2 * A_HEADS
    pr = PAGE * A_KVH
    ck = cache_k.reshape(cache_k.shape[0], cache_k.shape[1], pr, 2 * A_HD)
    cv = cache_v.reshape(cache_v.shape[0], cache_v.shape[1], pr, A_VD)

    def flat(b, h, i):
        return (b * A_KVH + h) * nq + i

    def page_spec(p):
        def index(b, h, i, pt):
            f = flat(b, h, i)
            return (layer, pt[f // spt, (f % spt) * npg + p], 0, 0)
        return pl.BlockSpec((None, None, pr, A_VD), index)

    const2 = lambda b, h, i, pt: (0, 0)
    per_tok = lambda b, h, i, pt: (flat(b, h, i) // spt, 0, 0)
    grid_spec = pltpu.PrefetchScalarGridSpec(
        num_scalar_prefetch=1,
        grid=(batch, A_KVH, nq),
        in_specs=[
            pl.BlockSpec(memory_space=pltpu.SMEM),
            pl.BlockSpec((4, A_HD), const2),
            pl.BlockSpec((A_VD, 1), const2),
            pl.BlockSpec((1, A_VD), const2),
            pl.BlockSpec((t, 2 * A_VD), lambda b, h, i, pt: (b * nq + i, h)),
            pl.BlockSpec((seq, 2 * A_HD), lambda b, h, i, pt: (b, kcol0 + h)),
            pl.BlockSpec((None, VT_ROWS, seq), lambda b, h, i, pt: (h, 0, b)),
            pl.BlockSpec((2, None, 2, t, 2 * t), lambda b, h, i, pt: (0, h, 0, 0, 0)),
            pl.BlockSpec((None, A_HEADS, 2 * A_HD), per_tok),
            pl.BlockSpec((None, rows, 2 * A_HD), per_tok),
            pl.BlockSpec((None, rows, A_VD), per_tok),
            pl.BlockSpec((rows, pr), const2),
            pl.BlockSpec((rows, pr), const2),
            pl.BlockSpec((rows, 1), const2),
        ] + [page_spec(p) for p in range(npg)] * 2,
        out_specs=[
            pl.BlockSpec((t, 2 * A_VD), lambda b, h, i, pt: (b * nq + i, h)),
            pl.BlockSpec((None, A_HEADS, A_VD), per_tok),
        ],
        scratch_shapes=[
            pltpu.VMEM((2, 2 * t, 2 * A_HD), BF16),
            pltpu.VMEM((2, 1, 2 * t), F32),
            pltpu.VMEM((2, VT_ROWS, 2 * t), F32),
            pltpu.VMEM((rows, 2 * A_HD), BF16),
            pltpu.VMEM((rows, 1), F32),
            pltpu.VMEM((rows, 1), F32),
            pltpu.VMEM((rows, A_VD), F32),
        ],
    )
    return pl.pallas_call(
        functools.partial(_attn_a_kernel, t=t, lam_init=lam_init, npg=npg, steps_per_token=spt),
        grid_spec=grid_spec,
        out_shape=[jax.ShapeDtypeStruct((batch * seq, A_HEADS * A_VD), BF16),
                   jax.ShapeDtypeStruct((nb, A_HEADS, A_VD), F32)],
        compiler_params=_cparams(("arbitrary",) * 3),
        name="attn_a",
    )(page_table, rel_bias, lam_p, sub_g.reshape(A_VD, 1), sub_g.reshape(1, A_VD), qkv, qkv, vt, bias_near,
      q_s, knew_rows, vnew_rows, pb_far, pb_last, pb_self, *([ck] * npg), *([cv] * npg))


def _swa_dec_kernel(q_ref, kbuf_ref, vbuf_ref, knew_ref, vnew_ref, bias_ref, sink_ref, o_ref):
    nt = (((1,), (1,)), ((), ()))
    q = q_ref[...]
    kbuf = kbuf_ref[...].astype(BF16)
    vbuf = vbuf_ref[...].astype(BF16)
    knew = knew_ref[...]
    vnew = vnew_ref[...]
    bias = bias_ref[...]
    sinks = sink_ref[...]
    for kvh in range(B_KVH):
        rows = slice(kvh * B_GROUP, (kvh + 1) * B_GROUP)
        cols = slice(kvh * B_HD, (kvh + 1) * B_HD)
        qh = q[rows, :]
        s = lax.dot_general(qh, kbuf[:, cols], nt, preferred_element_type=F32) + bias[rows, 0:WINDOW]
        s_new = (jnp.sum(qh.astype(F32) * knew[:, cols], axis=-1, keepdims=True)
                 + bias[rows, WINDOW:WINDOW + 1])
        sk = sinks[rows, :]
        mx = jnp.maximum(jnp.maximum(jnp.max(s, axis=-1, keepdims=True), s_new), sk)
        e = jnp.exp(s - mx)
        e_new = jnp.exp(s_new - mx)
        den = jnp.sum(e, axis=-1, keepdims=True) + e_new + jnp.exp(sk - mx)
        o = jnp.dot(e.astype(BF16), vbuf[:, cols], preferred_element_type=F32)
        o_ref[rows, :] = (o + e_new * vnew[:, cols]) / den


def _swa_decode(q, kbuf, vbuf, knew, vnew, bias_dec, sinks):
    nb = q.shape[0]
    per_b = lambda b: (b, 0, 0)
    return pl.pallas_call(
        _swa_dec_kernel,
        grid=(nb,),
        in_specs=[
            pl.BlockSpec((None, B_HEADS, B_HD), per_b),
            pl.BlockSpec((None, WINDOW, B_K), per_b),
            pl.BlockSpec((None, WINDOW, B_K), per_b),
            pl.BlockSpec((None, 1, B_K), per_b),
            pl.BlockSpec((None, 1, B_K), per_b),
            pl.BlockSpec(bias_dec.shape, lambda b: (0, 0)),
            pl.BlockSpec((B_HEADS, 1), lambda b: (0, 0)),
        ],
        out_specs=pl.BlockSpec((None, B_HEADS, B_HD), per_b),
        out_shape=jax.ShapeDtypeStruct((nb, B_HEADS, B_HD), F32),
        compiler_params=_cparams(("arbitrary",)),
        name="swa_decode",
    )(q, kbuf, vbuf, knew, vnew, bias_dec, sinks.reshape(B_HEADS, 1))


def _qk_gain_rows(qk_g, q_cols, kw, scale):
    gq = jnp.tile(qk_g[0].reshape(-1), q_cols // qk_g[0].size) * scale
    gk = jnp.tile(qk_g[1].reshape(-1), kw // qk_g[1].size)
    return jnp.concatenate([gq, gk, jnp.ones((kw,), F32)]).reshape(1, -1)


def kernel(x_prompt, x_sample, cache_k_a, cache_v_a, cache_win_k, cache_win_v, page_table, c_prompt, c_sample,
           rel_bias, w_ada, b_ada, g_attn, g_ffn, w_qkv_a, qk_g_a, lam_a, sub_g_a, w_o_a, w_qkv_b, qk_g_b,
           sinks_b, w_o_b, w_gate_up, w_down):
    batch, seq, _ = x_prompt.shape
    nb = x_sample.shape[0]
    t_attn = 256
    tm_p, tm_ffn, tf = 512, 512, 512

    n_c = nb + batch
    pad = (-n_c) % 8
    c_all = jnp.concatenate([c_sample, c_prompt, jnp.zeros((pad, D_MODEL), F32)], axis=0)
    mod = _ada_all(c_all, w_ada, b_ada)

    bias_near = _bias_near(rel_bias, t_attn)
    bias_swa = _bias_swa(rel_bias)
    bias_dec = _bias_dec(rel_bias)
    bias_dec_rows = bias_dec.reshape(A_HEADS, 2, 2 * WINDOW).transpose(1, 0, 2).reshape(2 * A_HEADS, 2 * WINDOW)
    row_kvh = (jnp.arange(2 * A_HEADS) % A_HEADS) // (A_HEADS // A_KVH)
    own = row_kvh[:, None] == (jnp.arange(PAGE * A_KVH) % A_KVH)[None, :]
    pb_far = jnp.where(own, bias_dec_rows[:, 0:1], NEG)
    pb_last = jnp.where(own, jnp.repeat(bias_dec_rows[:, 0:WINDOW], A_KVH, axis=1), NEG)
    pb_self = bias_dec_rows[:, WINDOW:WINDOW + 1]

    yp = x_prompt.reshape(batch * seq, D_MODEL)
    ys = x_sample.reshape(nb, D_MODEL)
    outs = {k: [] for k in ("kpa", "vpa", "ksa", "vsa", "kpb", "vpb", "ksb", "vsb")}

    w_qkv_a16, w_o_a16 = w_qkv_a.astype(BF16), w_o_a.astype(BF16)
    w_qkv_b16, w_o_b16 = w_qkv_b.astype(BF16), w_o_b.astype(BF16)
    w_gu16, w_dn16 = w_gate_up.astype(BF16), w_down.astype(BF16)

    for i in range(DEPTH):
        mod_p = mod[i, :, nb:nb + batch].transpose(1, 0, 2).reshape(batch, 6, 1, D_MODEL)
        mod_s = mod[i, :, :nb]
        g_a = g_attn[i].reshape(1, D_MODEL)
        g_f = g_ffn[i].reshape(1, D_MODEL)
        li = i // 2
        if i % 2 == 0:
            lam_init = 0.8 - 0.6 * math.exp(-0.3 * i)
            w_qkv, w_o = w_qkv_a16, w_o_a16
            gain = _qk_gain_rows(qk_g_a[li], A_Q, A_K, A_HD ** -0.5)
            gain_p = _qk_gain_rows(qk_g_a[li], A_Q, A_K, A_HD ** -0.5 * LOG2E)
            qkv_p, k_p, v_p, vt_p = _qkv_proj(yp, g_a, mod_p, w_qkv, li, gain_p, tm=tm_p, tn=A_K, kw=A_K,
                                              rows_per_batch=seq, emit_vt=True)
            qkv_s, k_s, v_s = _qkv_proj(ys, g_a, mod_s, w_qkv, li, gain, tm=nb, tn=A_K, kw=A_K,
                                        rows_per_batch=None)
            kv_rows = lambda a: jnp.tile(jnp.repeat(a.reshape(nb, A_KVH, A_VD), A_HEADS // A_KVH, axis=1), (1, 2, 1))
            o_p, o_s = _attn_a(qkv_p, vt_p, bias_near, rel_bias, lam_a[li], sub_g_a[li],
                               qkv_s[:, :A_Q].reshape(nb, A_HEADS, 2 * A_HD), kv_rows(k_s), kv_rows(v_s),
                               cache_k_a, cache_v_a, li, page_table, pb_far, pb_last, pb_self,
                               batch=batch, seq=seq, t=t_attn, lam_init=lam_init)
            o_s = o_s.reshape(nb, A_HEADS * A_VD).astype(BF16)
            outs["kpa"].append(k_p.reshape(batch, seq // PAGE, PAGE, A_KVH, 2 * A_HD))
            outs["vpa"].append(v_p.reshape(batch, seq // PAGE, PAGE, A_KVH, A_VD))
            outs["ksa"].append(k_s.reshape(nb, 1, A_KVH, 2 * A_HD))
            outs["vsa"].append(v_s.reshape(nb, 1, A_KVH, A_VD))
        else:
            w_qkv, w_o = w_qkv_b16, w_o_b16
            gain = _qk_gain_rows(qk_g_b[li], B_Q, B_K, B_HD ** -0.5)
            qkv_p, k_p, v_p = _qkv_proj(yp, g_a, mod_p, w_qkv, li, gain, tm=tm_p, tn=2 * B_K, kw=B_K,
                                        rows_per_batch=seq)
            o_p = _swa_prompt(qkv_p, bias_swa, sinks_b[li], batch=batch, seq=seq)
            qkv_s, k_s, v_s = _qkv_proj(ys, g_a, mod_s, w_qkv, li, gain, tm=nb, tn=2 * B_K, kw=B_K,
                                        rows_per_batch=None)
            buf_k = cache_win_k[li].reshape(nb, WINDOW, B_K)
            buf_v = cache_win_v[li].reshape(nb, WINDOW, B_K)
            o_s = _swa_decode(qkv_s[:, :B_Q].reshape(nb, B_HEADS, B_HD), buf_k, buf_v, k_s.reshape(nb, 1, B_K),
                              v_s.reshape(nb, 1, B_K), bias_dec, sinks_b[li])
            o_s = o_s.reshape(nb, B_Q).astype(BF16)
            last_w = lambda a: a.reshape(batch, seq, B_K)[:, seq - WINDOW:].reshape(batch, WINDOW, B_KVH, B_HD)
            outs["kpb"].append(last_w(k_p))
            outs["vpb"].append(last_w(v_p))
            outs["ksb"].append(jnp.concatenate([buf_k[:, 1:], k_s.reshape(nb, 1, B_K)], axis=1)
                               .reshape(nb, WINDOW, B_KVH, B_HD))
            outs["vsb"].append(jnp.concatenate([buf_v[:, 1:], v_s.reshape(nb, 1, B_K)], axis=1)
                               .reshape(nb, WINDOW, B_KVH, B_HD))
        yp = _out_proj(o_p, w_o, li, yp, mod_p, tm=tm_p, rows_per_batch=seq)
        ys = _out_proj(o_s, w_o, li, ys, mod_s, tm=nb, rows_per_batch=None)
        yp = _ffn(yp, g_f, mod_p, w_gu16, w_dn16, i, tm=tm_ffn, tf=tf, rows_per_batch=seq)
        ys = _ffn(ys, g_f, mod_s, w_gu16, w_dn16, i, tm=nb, tf=tf, rows_per_batch=None)

    return (yp.reshape(batch, seq, D_MODEL), ys.reshape(nb, 1, D_MODEL),
            jnp.stack(outs["kpa"]), jnp.stack(outs["vpa"]), jnp.stack(outs["ksa"]), jnp.stack(outs["vsa"]),
            jnp.stack(outs["kpb"]), jnp.stack(outs["vpb"]), jnp.stack(outs["ksb"]), jnp.stack(outs["vsb"]))
```

```python
import functools
import math

import jax
import jax.numpy as jnp
import numpy as np
from jax import lax
from jax.experimental import pallas as pl
from jax.experimental.pallas import tpu as pltpu

F32 = jnp.float32
BF16 = jnp.bfloat16

D_MODEL = 2048
DEPTH = 4
PAGE = 128
A_HEADS, A_KVH, A_HD, A_VD = 16, 8, 64, 128
A_Q = A_HEADS * 2 * A_HD
A_K = A_KVH * 2 * A_HD
A_V = A_KVH * A_VD
B_HEADS, B_KVH, B_HD = 32, 4, 64
B_GROUP = B_HEADS // B_KVH
B_Q = B_HEADS * B_HD
B_K = B_KVH * B_HD
WINDOW = 128
D_FF = 5632
NUM_BUCKETS = 32
MAX_DISTANCE = 128
MAX_EXACT = NUM_BUCKETS // 2
RMS_EPS = 1e-6
LOG2E = math.log2(math.e)
VT_ROWS = A_VD + 16
NEG = -1e30
QK_GROUP = 64

VMEM_LIMIT = 56 * 1024 * 1024
PAGES_PER_SOFTMAX = 8


def _t5_saturation_distance():
    d = np.arange(0, 4 * MAX_DISTANCE, dtype=np.int32)
    nf = np.maximum(d, MAX_EXACT).astype(np.float32)
    large = MAX_EXACT + (np.log(nf / MAX_EXACT) / math.log(MAX_DISTANCE / MAX_EXACT)
                         * (NUM_BUCKETS - MAX_EXACT)).astype(np.int32)
    bucket = np.where(d < MAX_EXACT, d, np.minimum(large, NUM_BUCKETS - 1))
    below = np.nonzero(bucket < NUM_BUCKETS - 1)[0]
    return int(below.max()) + 1


T5_SAT_DIST = _t5_saturation_distance()
FAR_BUCKET = NUM_BUCKETS - 1


def _cparams(sem):
    return pltpu.CompilerParams(dimension_semantics=sem, vmem_limit_bytes=VMEM_LIMIT)


def _bucket(dist):
    n = jnp.maximum(dist, 0)
    nf = jnp.maximum(n, MAX_EXACT).astype(F32)
    large = MAX_EXACT + (jnp.log(nf / MAX_EXACT) / math.log(MAX_DISTANCE / MAX_EXACT)
                         * (NUM_BUCKETS - MAX_EXACT)).astype(jnp.int32)
    return jnp.where(n < MAX_EXACT, n, jnp.minimum(large, NUM_BUCKETS - 1))


def _table_lookup(bucket, tab_ref, col):
    acc = jnp.zeros(bucket.shape, F32)
    for b in range(NUM_BUCKETS):
        acc = jnp.where(bucket == b, tab_ref[b, col], acc)
    return acc


def _ada_kernel(c_ref, w_ref, b_ref, o_ref):
    c = c_ref[...]
    a = (c * jax.nn.sigmoid(c)).astype(BF16)
    o_ref[...] = jnp.dot(a, w_ref[...].astype(BF16), preferred_element_type=F32) + b_ref[...]


def _ada_all(c_all, w_ada, b_ada, tn=1024):
    mp = c_all.shape[0]
    per_chunk = D_MODEL // tn
    return pl.pallas_call(
        _ada_kernel,
        grid=(DEPTH, 6 * per_chunk),
        in_specs=[
            pl.BlockSpec((mp, D_MODEL), lambda l, j: (0, 0)),
            pl.BlockSpec((None, D_MODEL, tn), lambda l, j: (l, 0, j)),
            pl.BlockSpec((None, 1, tn), lambda l, j: (l, 0, j)),
        ],
        out_specs=pl.BlockSpec((None, None, mp, tn), lambda l, j: (l, j // per_chunk, 0, j % per_chunk)),
        out_shape=jax.ShapeDtypeStruct((DEPTH, 6, mp, D_MODEL), F32),
        compiler_params=_cparams(("arbitrary", "arbitrary")),
        name="ada_mod",
    )(c_all, w_ada, b_ada.reshape(DEPTH, 1, 6 * D_MODEL))


def _bias_near_kernel(tab_ref, o_ref, *, t):
    kind = pl.program_id(0)
    kvh = pl.program_id(1)
    m = pl.program_id(2)
    key = lax.broadcasted_iota(jnp.int32, (t, t), 0)
    qry = lax.broadcasted_iota(jnp.int32, (t, t), 1)
    dist = qry - key + (1 - kind) * t
    bucket = _bucket(dist)
    for g in range(2):
        col = kvh * 4 + g * 2 + m
        tile = _table_lookup(bucket, tab_ref, col)
        o_ref[:, g * t:(g + 1) * t] = jnp.where(dist >= 0, tile * LOG2E, NEG)


def _bias_near(rel_bias, t):
    return pl.pallas_call(
        functools.partial(_bias_near_kernel, t=t),
        grid=(2, A_KVH, 2),
        in_specs=[pl.BlockSpec(memory_space=pltpu.SMEM)],
        out_specs=pl.BlockSpec((None, None, None, t, 2 * t), lambda k, h, m: (k, h, m, 0, 0)),
        out_shape=jax.ShapeDtypeStruct((2, A_KVH, 2, t, 2 * t), F32),
        compiler_params=_cparams(("arbitrary",) * 3),
        name="bias_near",
    )(rel_bias)


def _bias_swa_kernel(tab_ref, o_ref):
    first = pl.program_id(0)
    kvh = pl.program_id(1)
    q = lax.broadcasted_iota(jnp.int32, (WINDOW, 2 * WINDOW), 0)
    s = lax.broadcasted_iota(jnp.int32, (WINDOW, 2 * WINDOW), 1)
    dist = q + WINDOW - s
    bucket = _bucket(dist)
    valid = (dist >= 0) & (dist <= WINDOW) & (s >= first * WINDOW)
    for g in range(B_GROUP):
        tile = _table_lookup(bucket, tab_ref, kvh * B_GROUP + g)
        o_ref[g * WINDOW:(g + 1) * WINDOW, :] = jnp.where(valid, tile, NEG)


def _bias_swa(rel_bias):
    return pl.pallas_call(
        _bias_swa_kernel,
        grid=(2, B_KVH),
        in_specs=[pl.BlockSpec(memory_space=pltpu.SMEM)],
        out_specs=pl.BlockSpec((None, None, B_GROUP * WINDOW, 2 * WINDOW), lambda f, h: (f, h, 0, 0)),
        out_shape=jax.ShapeDtypeStruct((2, B_KVH, B_GROUP * WINDOW, 2 * WINDOW), F32),
        compiler_params=_cparams(("arbitrary", "arbitrary")),
        name="bias_swa",
    )(rel_bias)


def _bias_dec_kernel(tab_t_ref, o_ref):
    s = lax.broadcasted_iota(jnp.int32, (NUM_BUCKETS, 2 * WINDOW), 1)
    bucket = _bucket(WINDOW - s)
    tab_t = tab_t_ref[...]
    acc = jnp.zeros((NUM_BUCKETS, 2 * WINDOW), F32)
    for b in range(NUM_BUCKETS):
        acc = jnp.where(bucket == b, tab_t[:, b:b + 1], acc)
    o_ref[...] = acc


def _bias_dec(rel_bias):
    return pl.pallas_call(
        _bias_dec_kernel,
        out_shape=jax.ShapeDtypeStruct((NUM_BUCKETS, 2 * WINDOW), F32),
        name="bias_dec",
    )(rel_bias.T)


def _modulate(x, g, shift, scale):
    ms = jnp.mean(x * x, axis=-1, keepdims=True)
    h = x * lax.rsqrt(ms + RMS_EPS) * g
    return h * (1.0 + scale) + shift


def _mod_specs(mod, chunks, tm, rows_per_batch):
    specs = []
    for c in chunks:
        if mod.ndim == 4:
            tiles = rows_per_batch // tm
            specs.append(pl.BlockSpec((None, None, 1, D_MODEL),
                                      lambda i, *_, c=c, tiles=tiles: (i // tiles, c, 0, 0)))
        else:
            specs.append(pl.BlockSpec((None, tm, D_MODEL), lambda i, *_, c=c: (c, i, 0)))
    return specs


def _qkv_kernel(x_ref, g_ref, shift_ref, scale_ref, w_ref, gain_ref, p_ref, ob_ref, k_ref, v_ref, *rest,
                tn, kw, n_q_tiles, n_tiles, emit_vt):
    vt_ref = rest[0] if emit_vt else None
    h_ref = rest[-1]
    j = pl.program_id(1)
    pw = p_ref.shape[0]

    @pl.when(j == 0)
    def _():
        h_ref[...] = _modulate(x_ref[...], g_ref[...], shift_ref[...], scale_ref[...]).astype(BF16)

    def tile(parts):
        h = h_ref[...]
        proj = lambda c: jnp.dot(h, w_ref[:, c * pw:(c + 1) * pw], preferred_element_type=F32)
        nxt = proj(0)
        for c, (kind, off) in enumerate(parts):
            cols = slice(c * pw, (c + 1) * pw)
            out = nxt
            if c + 1 < len(parts):
                nxt = proj(c + 1)
            if kind != "v":
                ms = jnp.dot((out * out).astype(BF16), p_ref[...], preferred_element_type=F32)
                out = out * lax.rsqrt(ms + RMS_EPS) * gain_ref[:, cols]
            ob_ref[:, cols] = out.astype(BF16)
            if kind == "k":
                k_ref[:, off:off + pw] = out
            if kind == "v":
                v_ref[:, off:off + pw] = out
                if emit_vt:
                    vt = out.T.astype(BF16)
                    for i in range(pw // A_VD):
                        head = off // A_VD + i
                        vt_ref[head, 0:A_VD, :] = vt[i * A_VD:(i + 1) * A_VD, :]
                        vt_ref[head, A_VD:VT_ROWS, :] = jnp.ones((VT_ROWS - A_VD, vt.shape[1]), BF16)

    @pl.when(j < n_q_tiles)
    def _():
        tile([("q", 0)] * (tn // pw))

    for jj in range(n_q_tiles, n_tiles):
        offs = [(jj - n_q_tiles) * tn + c * pw for c in range(tn // pw)]
        parts = [("k", o) if o < kw else ("v", o - kw) for o in offs]
        pl.when(j == jj)(functools.partial(tile, parts))


def _qkv_proj(x, g, mod, w, layer, gain, *, tm, tn, kw, rows_per_batch, emit_vt=False):
    m_rows, n = x.shape[0], w.shape[2]
    q_cols = n - 2 * kw
    pw = 256
    assert q_cols % tn == 0 and tn % pw == 0 and kw % pw == 0
    idx = np.arange(pw)
    pmat = jnp.asarray((idx[:, None] // QK_GROUP == idx[None, :] // QK_GROUP) / QK_GROUP, BF16)
    shift_spec, scale_spec = _mod_specs(mod, (0, 1), tm, rows_per_batch)
    return pl.pallas_call(
        functools.partial(_qkv_kernel, tn=tn, kw=kw, n_q_tiles=q_cols // tn, n_tiles=n // tn, emit_vt=emit_vt),
        grid=(m_rows // tm, n // tn),
        in_specs=[
            pl.BlockSpec((tm, D_MODEL), lambda i, j: (i, 0)),
            pl.BlockSpec((1, D_MODEL), lambda i, j: (0, 0)),
            shift_spec, scale_spec,
            pl.BlockSpec((None, D_MODEL, tn), lambda i, j: (layer, 0, j)),
            pl.BlockSpec((1, tn), lambda i, j: (0, j)),
            pl.BlockSpec((pw, pw), lambda i, j: (0, 0)),
        ],
        out_specs=[
            pl.BlockSpec((tm, tn), lambda i, j: (i, j)),
            pl.BlockSpec((tm, kw), lambda i, j: (i, 0)),
            pl.BlockSpec((tm, kw), lambda i, j: (i, 0)),
        ] + ([pl.BlockSpec((A_KVH, VT_ROWS, tm), lambda i, j: (0, 0, i))] if emit_vt else []),
        out_shape=[
            jax.ShapeDtypeStruct((m_rows, n), BF16),
            jax.ShapeDtypeStruct((m_rows, kw), F32),
            jax.ShapeDtypeStruct((m_rows, kw), F32),
        ] + ([jax.ShapeDtypeStruct((A_KVH, VT_ROWS, m_rows), BF16)] if emit_vt else []),
        scratch_shapes=[pltpu.VMEM((tm, D_MODEL), BF16)],
        compiler_params=_cparams(("arbitrary", "arbitrary")),
        name="qkv_proj",
    )(x, g, mod, mod, w, gain, pmat)


def _oproj_kernel(o_ref, w_ref, y_ref, gate_ref, out_ref):
    out_ref[...] = y_ref[...] + gate_ref[...] * jnp.dot(o_ref[...], w_ref[...], preferred_element_type=F32)


def _out_proj(o, w, layer, y, mod, *, tm, rows_per_batch):
    m_rows = y.shape[0]
    (gate_spec,) = _mod_specs(mod, (2,), tm, rows_per_batch)
    return pl.pallas_call(
        _oproj_kernel,
        grid=(m_rows // tm,),
        in_specs=[
            pl.BlockSpec((tm, o.shape[1]), lambda i: (i, 0)),
            pl.BlockSpec((None,) + w.shape[1:], lambda i: (layer, 0, 0)),
            pl.BlockSpec((tm, D_MODEL), lambda i: (i, 0)),
            gate_spec,
        ],
        out_specs=pl.BlockSpec((tm, D_MODEL), lambda i: (i, 0)),
        out_shape=jax.ShapeDtypeStruct((m_rows, D_MODEL), F32),
        compiler_params=_cparams(("arbitrary",)),
        name="out_proj",
    )(o, w, y, mod)


def _ffn_kernel(y_ref, g_ref, shift_ref, scale_ref, gate_ref, wg_ref, wu_ref, wd_ref, out_ref, *rest, emit_w16):
    h_ref, acc_ref = rest[-2:]
    j = pl.program_id(1)

    @pl.when(j == 0)
    def _():
        h_ref[...] = _modulate(y_ref[...], g_ref[...], shift_ref[...], scale_ref[...]).astype(BF16)
        acc_ref[...] = jnp.zeros_like(acc_ref)

    wg, wu, wd = wg_ref[...].astype(BF16), wu_ref[...].astype(BF16), wd_ref[...].astype(BF16)
    if emit_w16:
        rest[0][...], rest[1][...], rest[2][...] = wg, wu, wd
    h = h_ref[...]
    gate = jnp.dot(h, wg, preferred_element_type=F32)
    up = jnp.dot(h, wu, preferred_element_type=F32)
    act = (gate * jax.nn.sigmoid(gate) * up).astype(BF16)
    acc_ref[...] += jnp.dot(act, wd, preferred_element_type=F32)

    @pl.when(j == pl.num_programs(1) - 1)
    def _():
        out_ref[...] = y_ref[...] + gate_ref[...] * acc_ref[...]


def _ffn(y, g, mod, w_gate, w_up, w_down, layer, *, tm, tf, rows_per_batch, emit_w16=False):
    m_rows = y.shape[0]
    nf = D_FF // tf
    shift_spec, scale_spec, gate_spec = _mod_specs(mod, (3, 4, 5), tm, rows_per_batch)
    if emit_w16:
        assert m_rows == tm
        w_specs = [pl.BlockSpec((None, D_MODEL, tf), lambda i, j: (layer, 0, j)),
                   pl.BlockSpec((None, D_MODEL, tf), lambda i, j: (layer, 0, j + nf)),
                   pl.BlockSpec((None, tf, D_MODEL), lambda i, j: (layer, j, 0))]
    else:
        w_specs = [pl.BlockSpec((D_MODEL, tf), lambda i, j: (0, j)),
                   pl.BlockSpec((D_MODEL, tf), lambda i, j: (0, j)),
                   pl.BlockSpec((tf, D_MODEL), lambda i, j: (j, 0))]
    w16_specs = [pl.BlockSpec((D_MODEL, tf), lambda i, j: (0, j)),
                 pl.BlockSpec((D_MODEL, tf), lambda i, j: (0, j)),
                 pl.BlockSpec((tf, D_MODEL), lambda i, j: (j, 0))]
    w16_shapes = [jax.ShapeDtypeStruct((D_MODEL, D_FF), BF16), jax.ShapeDtypeStruct((D_MODEL, D_FF), BF16),
                  jax.ShapeDtypeStruct((D_FF, D_MODEL), BF16)]
    return pl.pallas_call(
        functools.partial(_ffn_kernel, emit_w16=emit_w16),
        grid=(m_rows // tm, nf),
        in_specs=[
            pl.BlockSpec((tm, D_MODEL), lambda i, j: (i, 0)),
            pl.BlockSpec((1, D_MODEL), lambda i, j: (0, 0)),
            shift_spec, scale_spec, gate_spec,
        ] + w_specs,
        out_specs=[pl.BlockSpec((tm, D_MODEL), lambda i, j: (i, 0))] + (w16_specs if emit_w16 else []),
        out_shape=[jax.ShapeDtypeStruct((m_rows, D_MODEL), F32)] + (w16_shapes if emit_w16 else []),
        scratch_shapes=[pltpu.VMEM((tm, D_MODEL), BF16), pltpu.VMEM((tm, D_MODEL), F32)],
        compiler_params=_cparams(("arbitrary", "arbitrary")),
        name="ffn",
    )(y, g, mod, mod, mod, w_gate, w_up, w_down)


def _softmax_step(s, vt, m_ref, acc_ref, mi):
    m_old = m_ref[mi]
    m_new = jnp.maximum(m_old, jnp.max(s, axis=0, keepdims=True))
    p = jnp.exp2(s - m_new)
    acc_ref[mi] = jnp.exp2(m_old - m_new) * acc_ref[mi] + jnp.dot(vt, p.astype(BF16), preferred_element_type=F32)
    m_ref[mi] = m_new


def _diff_lambda(lam_ref, lam_init):
    lp = lam_ref[...]
    a = jnp.sum(lp[0:1] * lp[1:2], axis=-1, keepdims=True)
    b = jnp.sum(lp[2:3] * lp[3:4], axis=-1, keepdims=True)
    return jnp.exp(a) - jnp.exp(b) + lam_init


def _head_rms(o, subg, lam_init, axis):
    ms = jnp.mean(o * o, axis=axis, keepdims=True)
    return o * lax.rsqrt(ms + RMS_EPS) * subg * (1.0 - lam_init)


def _diff_flash_body(kvh, qi, tab_ref, lam_ref, subg_ref, q_ref, k_ref, vt_ref, bias_ref, o_ref,
                     qz_ref, m_ref, acc_ref, *, t, lam_init):
    nt = (((1,), (1,)), ((), ()))

    q = q_ref[...]
    lane = lax.broadcasted_iota(jnp.int32, (t, 2 * A_HD), 1)
    for g in range(2):
        qg = q[:, g * 2 * A_HD:(g + 1) * 2 * A_HD]
        qz_ref[0, g * t:(g + 1) * t, :] = jnp.where(lane < A_HD, qg, jnp.zeros_like(qg))
        qz_ref[1, g * t:(g + 1) * t, :] = jnp.where(lane >= A_HD, qg, jnp.zeros_like(qg))
    m_ref[...] = jnp.full(m_ref.shape, NEG, F32)
    acc_ref[...] = jnp.zeros(acc_ref.shape, F32)

    def scores(ki, kind):
        k = k_ref[pl.ds(pl.multiple_of(ki * t, t), t), :]
        ss = [lax.dot_general(k, qz_ref[mi], nt, preferred_element_type=F32) for mi in range(2)]
        return ss if kind is None else [bias_ref[kind, mi] + ss[mi] for mi in range(2)]

    def update(ki, ss):
        vt = vt_ref[:, pl.ds(pl.multiple_of(ki * t, t), t)]
        for mi in range(2):
            _softmax_step(ss[mi], vt, m_ref, acc_ref, mi)

    n_far = jnp.maximum(qi - 1, 0)

    def group(k0, kinds):
        ss = [scores(k0 + j, kind) for j, kind in enumerate(kinds)]
        for j in range(len(kinds)):
            update(k0 + j, ss[j])

    def far_quad(i, carry):
        group(4 * i, (None,) * 4)
        return carry

    lax.fori_loop(0, n_far // 4, far_quad, 0)
    rem = n_far % 4
    base = n_far - rem

    @pl.when(rem >= 2)
    def _():
        group(base, (None, None))

    @pl.when(rem % 2 == 1)
    def _():
        group(n_far - 1, (None,))

    col = lax.broadcasted_iota(jnp.int32, (1, 2 * t), 1)
    for mi in range(2):
        c0 = tab_ref[FAR_BUCKET, kvh * 4 + mi] * LOG2E
        c1 = tab_ref[FAR_BUCKET, kvh * 4 + 2 + mi] * LOG2E
        m_ref[mi] = m_ref[mi] + jnp.where(col < t, c0, c1)

    @pl.when(qi > 0)
    def _():
        group(qi - 1, (0, 1))

    @pl.when(qi == 0)
    def _():
        group(qi, (1,))

    o0 = acc_ref[0, 0:A_VD, :] * (1.0 / acc_ref[0, A_VD:A_VD + 1, :])
    o1 = acc_ref[1, 0:A_VD, :] * (1.0 / acc_ref[1, A_VD:A_VD + 1, :])
    o = _head_rms(o0 - _diff_lambda(lam_ref, lam_init) * o1, subg_ref[...], lam_init, axis=0).T
    for g in range(2):
        o_ref[:, g * A_VD:(g + 1) * A_VD] = o[g * t:(g + 1) * t, :].astype(BF16)


def _swa_kernel(sink_ref, q_ref, kp_ref, kc_ref, vp_ref, vc_ref, bias_ref, o_ref):
    w = WINDOW
    nt = (((1,), (1,)), ((), ()))
    q = q_ref[...]
    lane = lax.broadcasted_iota(jnp.int32, (w, 2 * B_HD), 1)
    grp = lax.broadcasted_iota(jnp.int32, (B_GROUP * w, 1), 0) // w
    k_band = jnp.concatenate([kp_ref[...], kc_ref[...]], axis=0)
    v_band = jnp.concatenate([vp_ref[...], vc_ref[...]], axis=0)
    for kvh in range(B_KVH):
        cols = slice(kvh * B_HD, (kvh + 1) * B_HD)
        kk = k_band[:, cols]
        vv = v_band[:, cols]
        kk = jnp.concatenate([kk, kk], axis=1)
        vv = jnp.concatenate([vv, vv], axis=1)
        parts = []
        for j in range(B_GROUP // 2):
            chunk = q[:, (kvh * B_GROUP // 2 + j) * 2 * B_HD:(kvh * B_GROUP // 2 + j + 1) * 2 * B_HD]
            parts.append(jnp.where(lane < B_HD, chunk, jnp.zeros_like(chunk)))
            parts.append(jnp.where(lane >= B_HD, chunk, jnp.zeros_like(chunk)))
        qs = jnp.concatenate(parts, axis=0)
        s = lax.dot_general(qs, kk, nt, preferred_element_type=F32) + bias_ref[kvh]
        sk = jnp.zeros((B_GROUP * w, 1), F32)
        for g in range(B_GROUP):
            sk = jnp.where(grp == g, sink_ref[kvh * B_GROUP + g], sk)
        mx = jnp.maximum(jnp.max(s, axis=-1, keepdims=True), sk)
        e = jnp.exp(s - mx)
        den = jnp.sum(e, axis=-1, keepdims=True) + jnp.exp(sk - mx)
        o = jnp.dot(e.astype(BF16), vv, preferred_element_type=F32) * (1.0 / den)
        for j in range(B_GROUP // 2):
            oa = o[(2 * j) * w:(2 * j + 1) * w, :]
            ob = o[(2 * j + 1) * w:(2 * j + 2) * w, :]
            c0 = (kvh * B_GROUP // 2 + j) * 2 * B_HD
            o_ref[:, c0:c0 + 2 * B_HD] = jnp.where(lane < B_HD, oa, ob).astype(BF16)


def _swa_prompt(qkv, bias_swa, sinks, *, batch, seq):
    nb = seq // WINDOW
    kblk = B_Q // B_K
    cur = lambda b, n: b * nb + n
    prev = lambda b, n: b * nb + jnp.maximum(n - 1, 0)
    return pl.pallas_call(
        _swa_kernel,
        grid=(batch, nb),
        in_specs=[
            pl.BlockSpec(memory_space=pltpu.SMEM),
            pl.BlockSpec((WINDOW, B_Q), lambda b, n: (cur(b, n), 0)),
            pl.BlockSpec((WINDOW, B_K), lambda b, n: (prev(b, n), kblk)),
            pl.BlockSpec((WINDOW, B_K), lambda b, n: (cur(b, n), kblk)),
            pl.BlockSpec((WINDOW, B_K), lambda b, n: (prev(b, n), kblk + 1)),
            pl.BlockSpec((WINDOW, B_K), lambda b, n: (cur(b, n), kblk + 1)),
            pl.BlockSpec((None,) + bias_swa.shape[1:], lambda b, n: (jnp.where(n == 0, 1, 0), 0, 0, 0)),
        ],
        out_specs=pl.BlockSpec((WINDOW, B_Q), lambda b, n: (cur(b, n), 0)),
        out_shape=jax.ShapeDtypeStruct((batch * seq, B_Q), BF16),
        compiler_params=_cparams(("arbitrary", "arbitrary")),
        name="swa_prompt",
    )(sinks, qkv, qkv, qkv, qkv, qkv, bias_swa)


def _paged_body(jg, last, q_ref, knew_ref, vnew_ref, bias_far_ref, bias_last_ref, bias_self_ref, lam_ref, subg_ref,
                k_refs, v_refs, o_ref, qb_ref, m_ref, l_ref, acc_ref, *, lam_init):
    npg = len(k_refs)
    nt = (((1,), (1,)), ((), ()))
    rows = 2 * A_HEADS
    pr = PAGE * A_KVH

    @pl.when(jg == 0)
    def _():
        q = q_ref[...]
        lane = lax.broadcasted_iota(jnp.int32, q.shape, 1)
        qb_ref[...] = jnp.concatenate([jnp.where(lane < A_HD, q, jnp.zeros_like(q)),
                                       jnp.where(lane >= A_HD, q, jnp.zeros_like(q))], axis=0)
        m_ref[...] = jnp.full(m_ref.shape, NEG, F32)
        l_ref[...] = jnp.zeros(l_ref.shape, F32)
        acc_ref[...] = jnp.zeros(acc_ref.shape, F32)

    qb = qb_ref[...]
    bias_far = bias_far_ref[...]
    bias_end = jnp.where(jg == last, bias_last_ref[...], bias_far)
    sub = min(npg, PAGES_PER_SOFTMAX)

    def scores(g):
        return jnp.concatenate(
            [lax.dot_general(qb, k_refs[p][...].astype(BF16), nt, preferred_element_type=F32)
             + (bias_end if p == npg - 1 else bias_far) for p in range(g * sub, (g + 1) * sub)], axis=1)

    m, l, acc = m_ref[...], l_ref[...], acc_ref[...]
    nxt = scores(0)
    for g in range(npg // sub):
        s = nxt
        if g + 1 < npg // sub:
            nxt = scores(g + 1)
        m_new = jnp.maximum(m, jnp.max(s, axis=-1, keepdims=True))
        pf = jnp.exp(s - m_new)
        alpha = jnp.exp(m - m_new)
        l = alpha * l + jnp.sum(pf, axis=-1, keepdims=True)
        p = pf.astype(BF16)
        pv = jnp.dot(p[:, 0:pr], v_refs[g * sub][...].astype(BF16), preferred_element_type=F32)
        for i in range(1, sub):
            pv = pv + jnp.dot(p[:, i * pr:(i + 1) * pr], v_refs[g * sub + i][...].astype(BF16),
                              preferred_element_type=F32)
        acc = alpha * acc + pv
        m = m_new
    m_ref[...], l_ref[...], acc_ref[...] = m, l, acc

    @pl.when(jg == last)
    def _():
        s_self = jnp.sum(qb.astype(F32) * knew_ref[...], axis=-1, keepdims=True) + bias_self_ref[...]
        m_o = m_ref[...]
        m_n = jnp.maximum(m_o, s_self)
        p_self = jnp.exp(s_self - m_n)
        a = jnp.exp(m_o - m_n)
        l = a * l_ref[...] + p_self
        o = (a * acc_ref[...] + p_self * vnew_ref[...]) / l
        od = o[0:A_HEADS, :] - _diff_lambda(lam_ref, lam_init) * o[A_HEADS:rows, :]
        o_ref[...] = _head_rms(od, subg_ref[...], lam_init, axis=-1)


def _attn_a_kernel(pt_ref, tab_ref, lam_ref, subg_col_ref, subg_row_ref, q_ref, k_ref, vt_ref, bias_ref,
                   qs_ref, knew_ref, vnew_ref, pb_far_ref, pb_last_ref, pb_self_ref, *rest,
                   t, lam_init, npg, steps_per_token):
    del pt_ref
    k_pages, v_pages = rest[:npg], rest[npg:2 * npg]
    o_ref, os_ref, qz_ref, m_ref, acc_ref, qb_ref, dm_ref, dl_ref, dacc_ref = rest[2 * npg:]
    kvh = pl.program_id(1)
    qi = pl.program_id(2)
    flat = (pl.program_id(0) * pl.num_programs(1) + kvh) * pl.num_programs(2) + qi
    _paged_body(flat % steps_per_token, steps_per_token - 1, qs_ref, knew_ref, vnew_ref, pb_far_ref, pb_last_ref,
                pb_self_ref, lam_ref, subg_row_ref, k_pages, v_pages, os_ref, qb_ref, dm_ref, dl_ref, dacc_ref,
                lam_init=lam_init)
    _diff_flash_body(kvh, qi, tab_ref, lam_ref, subg_col_ref, q_ref, k_ref, vt_ref, bias_ref, o_ref,
                     qz_ref, m_ref, acc_ref, t=t, lam_init=lam_init)


def _attn_a(qkv, vt, bias_near, rel_bias, lam_p, sub_g, q_s, knew_rows, vnew_rows, cache_k, cache_v, layer,
            page_table, pb_far, pb_last, pb_self, *, batch, seq, t, lam_init):
    assert t + 1 >= T5_SAT_DIST
    nb, n_pages = page_table.shape
    assert PAGE * (n_pages - 1) + 1 >= T5_SAT_DIST and PAGE == WINDOW
    nq = seq // t
    n_steps = batch * A_KVH * nq
    npg = nb * n_pages // n_steps
    assert npg * n_steps == nb * n_pages and n_pages % npg == 0 and npg % min(npg, PAGES_PER_SOFTMAX) == 0
    spt = n_pages // npg
    kcol0 = A_Q // (2 * A_HD)
    rows = 2 * A_HEADS
    pr = PAGE * A_KVH
    ck = cache_k.reshape(cache_k.shape[0], cache_k.shape[1], pr, 2 * A_HD)
    cv = cache_v.reshape(cache_v.shape[0], cache_v.shape[1], pr, A_VD)

    def flat(b, h, i):
        return (b * A_KVH + h) * nq + i

    def page_spec(p):
        def index(b, h, i, pt):
            f = flat(b, h, i)
            return (layer, pt[f // spt, (f % spt) * npg + p], 0, 0)
        return pl.BlockSpec((None, None, pr, A_VD), index)

    const2 = lambda b, h, i, pt: (0, 0)
    per_tok = lambda b, h, i, pt: (flat(b, h, i) // spt, 0, 0)
    grid_spec = pltpu.PrefetchScalarGridSpec(
        num_scalar_prefetch=1,
        grid=(batch, A_KVH, nq),
        in_specs=[
            pl.BlockSpec(memory_space=pltpu.SMEM),
            pl.BlockSpec((4, A_HD), const2),
            pl.BlockSpec((A_VD, 1), const2),
            pl.BlockSpec((1, A_VD), const2),
            pl.BlockSpec((t, 2 * A_VD), lambda b, h, i, pt: (b * nq + i, h)),
            pl.BlockSpec((seq, 2 * A_HD), lambda b, h, i, pt: (b, kcol0 + h)),
            pl.BlockSpec((None, VT_ROWS, seq), lambda b, h, i, pt: (h, 0, b)),
            pl.BlockSpec((2, None, 2, t, 2 * t), lambda b, h, i, pt: (0, h, 0, 0, 0)),
            pl.BlockSpec((None, A_HEADS, 2 * A_HD), per_tok),
            pl.BlockSpec((None, rows, 2 * A_HD), per_tok),
            pl.BlockSpec((None, rows, A_VD), per_tok),
            pl.BlockSpec((rows, pr), const2),
            pl.BlockSpec((rows, pr), const2),
            pl.BlockSpec((rows, 1), const2),
        ] + [page_spec(p) for p in range(npg)] * 2,
        out_specs=[
            pl.BlockSpec((t, 2 * A_VD), lambda b, h, i, pt: (b * nq + i, h)),
            pl.BlockSpec((None, A_HEADS, A_VD), per_tok),
        ],
        scratch_shapes=[
            pltpu.VMEM((2, 2 * t, 2 * A_HD), BF16),
            pltpu.VMEM((2, 1, 2 * t), F32),
            pltpu.VMEM((2, VT_ROWS, 2 * t), F32),
            pltpu.VMEM((rows, 2 * A_HD), BF16),
            pltpu.VMEM((rows, 1), F32),
            pltpu.VMEM((rows, 1), F32),
            pltpu.VMEM((rows, A_VD), F32),
        ],
    )
    return pl.pallas_call(
        functools.partial(_attn_a_kernel, t=t, lam_init=lam_init, npg=npg, steps_per_token=spt),
        grid_spec=grid_spec,
        out_shape=[jax.ShapeDtypeStruct((batch * seq, A_HEADS * A_VD), BF16),
                   jax.ShapeDtypeStruct((nb, A_HEADS, A_VD), F32)],
        compiler_params=_cparams(("arbitrary",) * 3),
        name="attn_a",
    )(page_table, rel_bias, lam_p, sub_g.reshape(A_VD, 1), sub_g.reshape(1, A_VD), qkv, qkv, vt, bias_near,
      q_s, knew_rows, vnew_rows, pb_far, pb_last, pb_self, *([ck] * npg), *([cv] * npg))


def _swa_dec_kernel(q_ref, kbuf_ref, vbuf_ref, knew_ref, vnew_ref, bias_ref, sink_ref, o_ref):
    nt = (((1,), (1,)), ((), ()))
    q = q_ref[...]
    kbuf = kbuf_ref[...].astype(BF16)
    vbuf = vbuf_ref[...].astype(BF16)
    knew = knew_ref[...]
    vnew = vnew_ref[...]
    bias = bias_ref[...]
    sinks = sink_ref[...]
    for kvh in range(B_KVH):
        rows = slice(kvh * B_GROUP, (kvh + 1) * B_GROUP)
        cols = slice(kvh * B_HD, (kvh + 1) * B_HD)
        qh = q[rows, :]
        s = lax.dot_general(qh, kbuf[:, cols], nt, preferred_element_type=F32) + bias[rows, 0:WINDOW]
        s_new = (jnp.sum(qh.astype(F32) * knew[:, cols], axis=-1, keepdims=True)
                 + bias[rows, WINDOW:WINDOW + 1])
        sk = sinks[rows, :]
        mx = jnp.maximum(jnp.maximum(jnp.max(s, axis=-1, keepdims=True), s_new), sk)
        e = jnp.exp(s - mx)
        e_new = jnp.exp(s_new - mx)
        den = jnp.sum(e, axis=-1, keepdims=True) + e_new + jnp.exp(sk - mx)
        o = jnp.dot(e.astype(BF16), vbuf[:, cols], preferred_element_type=F32)
        o_ref[rows, :] = (o + e_new * vnew[:, cols]) / den


def _swa_decode(q, kbuf, vbuf, knew, vnew, bias_dec, sinks):
    nb = q.shape[0]
    per_b = lambda b: (b, 0, 0)
    return pl.pallas_call(
        _swa_dec_kernel,
        grid=(nb,),
        in_specs=[
            pl.BlockSpec((None, B_HEADS, B_HD), per_b),
            pl.BlockSpec((None, WINDOW, B_K), per_b),
            pl.BlockSpec((None, WINDOW, B_K), per_b),
            pl.BlockSpec((None, 1, B_K), per_b),
            pl.BlockSpec((None, 1, B_K), per_b),
            pl.BlockSpec(bias_dec.shape, lambda b: (0, 0)),
            pl.BlockSpec((B_HEADS, 1), lambda b: (0, 0)),
        ],
        out_specs=pl.BlockSpec((None, B_HEADS, B_HD), per_b),
        out_shape=jax.ShapeDtypeStruct((nb, B_HEADS, B_HD), F32),
        compiler_params=_cparams(("arbitrary",)),
        name="swa_decode",
    )(q, kbuf, vbuf, knew, vnew, bias_dec, sinks.reshape(B_HEADS, 1))


def _qk_gain_rows(qk_g, q_cols, kw, scale):
    gq = jnp.tile(qk_g[0].reshape(-1), q_cols // qk_g[0].size) * scale
    gk = jnp.tile(qk_g[1].reshape(-1), kw // qk_g[1].size)
    return jnp.concatenate([gq, gk, jnp.ones((kw,), F32)]).reshape(1, -1)


def kernel(x_prompt, x_sample, cache_k_a, cache_v_a, cache_win_k, cache_win_v, page_table, c_prompt, c_sample,
           rel_bias, w_ada, b_ada, g_attn, g_ffn, w_qkv_a, qk_g_a, lam_a, sub_g_a, w_o_a, w_qkv_b, qk_g_b,
           sinks_b, w_o_b, w_gate_up, w_down):
    batch, seq, _ = x_prompt.shape
    nb = x_sample.shape[0]
    t_attn = 256
    tm_p, tm_ffn, tf = 512, 512, 512

    n_c = nb + batch
    pad = (-n_c) % 8
    c_all = jnp.concatenate([c_sample, c_prompt, jnp.zeros((pad, D_MODEL), F32)], axis=0)
    mod = _ada_all(c_all, w_ada, b_ada)

    bias_near = _bias_near(rel_bias, t_attn)
    bias_swa = _bias_swa(rel_bias)
    bias_dec = _bias_dec(rel_bias)
    bias_dec_rows = bias_dec.reshape(A_HEADS, 2, 2 * WINDOW).transpose(1, 0, 2).reshape(2 * A_HEADS, 2 * WINDOW)
    row_kvh = (jnp.arange(2 * A_HEADS) % A_HEADS) // (A_HEADS // A_KVH)
    own = row_kvh[:, None] == (jnp.arange(PAGE * A_KVH) % A_KVH)[None, :]
    pb_far = jnp.where(own, bias_dec_rows[:, 0:1], NEG)
    pb_last = jnp.where(own, jnp.repeat(bias_dec_rows[:, 0:WINDOW], A_KVH, axis=1), NEG)
    pb_self = bias_dec_rows[:, WINDOW:WINDOW + 1]

    yp = x_prompt.reshape(batch * seq, D_MODEL)
    ys = x_sample.reshape(nb, D_MODEL)
    outs = {k: [] for k in ("kpa", "vpa", "ksa", "vsa", "kpb", "vpb", "ksb", "vsb")}

    w_qkv_a16, w_o_a16 = w_qkv_a.astype(BF16), w_o_a.astype(BF16)
    w_qkv_b16, w_o_b16 = w_qkv_b.astype(BF16), w_o_b.astype(BF16)

    for i in range(DEPTH):
        mod_p = mod[i, :, nb:nb + batch].transpose(1, 0, 2).reshape(batch, 6, 1, D_MODEL)
        mod_s = mod[i, :, :nb]
        g_a = g_attn[i].reshape(1, D_MODEL)
        g_f = g_ffn[i].reshape(1, D_MODEL)
        li = i // 2
        if i % 2 == 0:
            lam_init = 0.8 - 0.6 * math.exp(-0.3 * i)
            w_qkv, w_o = w_qkv_a16, w_o_a16
            gain = _qk_gain_rows(qk_g_a[li], A_Q, A_K, A_HD ** -0.5)
            gain_p = _qk_gain_rows(qk_g_a[li], A_Q, A_K, A_HD ** -0.5 * LOG2E)
            qkv_p, k_p, v_p, vt_p = _qkv_proj(yp, g_a, mod_p, w_qkv, li, gain_p, tm=tm_p, tn=A_K, kw=A_K,
                                              rows_per_batch=seq, emit_vt=True)
            qkv_s, k_s, v_s = _qkv_proj(ys, g_a, mod_s, w_qkv, li, gain, tm=nb, tn=A_K, kw=A_K,
                                        rows_per_batch=None)
            kv_rows = lambda a: jnp.tile(jnp.repeat(a.reshape(nb, A_KVH, A_VD), A_HEADS // A_KVH, axis=1), (1, 2, 1))
            o_p, o_s = _attn_a(qkv_p, vt_p, bias_near, rel_bias, lam_a[li], sub_g_a[li],
                               qkv_s[:, :A_Q].reshape(nb, A_HEADS, 2 * A_HD), kv_rows(k_s), kv_rows(v_s),
                               cache_k_a, cache_v_a, li, page_table, pb_far, pb_last, pb_self,
                               batch=batch, seq=seq, t=t_attn, lam_init=lam_init)
            o_s = o_s.reshape(nb, A_HEADS * A_VD).astype(BF16)
            outs["kpa"].append(k_p.reshape(batch, seq // PAGE, PAGE, A_KVH, 2 * A_HD))
            outs["vpa"].append(v_p.reshape(batch, seq // PAGE, PAGE, A_KVH, A_VD))
            outs["ksa"].append(k_s.reshape(nb, 1, A_KVH, 2 * A_HD))
            outs["vsa"].append(v_s.reshape(nb, 1, A_KVH, A_VD))
        else:
            w_qkv, w_o = w_qkv_b16, w_o_b16
            gain = _qk_gain_rows(qk_g_b[li], B_Q, B_K, B_HD ** -0.5)
            qkv_p, k_p, v_p = _qkv_proj(yp, g_a, mod_p, w_qkv, li, gain, tm=tm_p, tn=2 * B_K, kw=B_K,
                                        rows_per_batch=seq)
            o_p = _swa_prompt(qkv_p, bias_swa, sinks_b[li], batch=batch, seq=seq)
            qkv_s, k_s, v_s = _qkv_proj(ys, g_a, mod_s, w_qkv, li, gain, tm=nb, tn=2 * B_K, kw=B_K,
                                        rows_per_batch=None)
            buf_k = cache_win_k[li].reshape(nb, WINDOW, B_K)
            buf_v = cache_win_v[li].reshape(nb, WINDOW, B_K)
            o_s = _swa_decode(qkv_s[:, :B_Q].reshape(nb, B_HEADS, B_HD), buf_k, buf_v, k_s.reshape(nb, 1, B_K),
                              v_s.reshape(nb, 1, B_K), bias_dec, sinks_b[li])
            o_s = o_s.reshape(nb, B_Q).astype(BF16)
            last_w = lambda a: a.reshape(batch, seq, B_K)[:, seq - WINDOW:].reshape(batch, WINDOW, B_KVH, B_HD)
            outs["kpb"].append(last_w(k_p))
            outs["vpb"].append(last_w(v_p))
            outs["ksb"].append(jnp.concatenate([buf_k[:, 1:], k_s.reshape(nb, 1, B_K)], axis=1)
                               .reshape(nb, WINDOW, B_KVH, B_HD))
            outs["vsb"].append(jnp.concatenate([buf_v[:, 1:], v_s.reshape(nb, 1, B_K)], axis=1)
                               .reshape(nb, WINDOW, B_KVH, B_HD))
        yp = _out_proj(o_p, w_o, li, yp, mod_p, tm=tm_p, rows_per_batch=seq)
        ys = _out_proj(o_s, w_o, li, ys, mod_s, tm=nb, rows_per_batch=None)
        ys, wg16, wu16, wd16 = _ffn(ys, g_f, mod_s, w_gate_up, w_gate_up, w_down, i, tm=nb, tf=tf,
                                    rows_per_batch=None, emit_w16=True)
        (yp,) = _ffn(yp, g_f, mod_p, wg16, wu16, wd16, i, tm=tm_ffn, tf=tf, rows_per_batch=seq)

    return (yp.reshape(batch, seq, D_MODEL), ys.reshape(nb, 1, D_MODEL),
            jnp.stack(outs["kpa"]), jnp.stack(outs["vpa"]), jnp.stack(outs["ksa"]), jnp.stack(outs["vsa"]),
            jnp.stack(outs["kpb"]), jnp.stack(outs["vpb"]), jnp.stack(outs["ksb"]), jnp.stack(outs["vsb"]))
```
